```python
import math
import jax, jax.numpy as jnp
from jax import lax
import numpy as np

D_MODEL = 2048
BATCH = 4
SEQ = 8192
DEPTH = 2
DEC_BATCH = 8
DEC_SEQ = 64
PAST_LEN = 2048

CHUNK = 64
N_META = 16
D_CONV_BR = 1024
CONV_A_W = 3
N_HEADS = 8
HEAD_DIM = 128
D_DELTA = N_HEADS * HEAD_DIM
D_QKV = 3 * D_DELTA
CONV_QKV_W = 4
D_MIX = D_CONV_BR + D_DELTA
PROJ_SIZES = (D_CONV_BR, D_CONV_BR, D_CONV_BR, D_CONV_BR, D_QKV, D_DELTA, N_HEADS, N_HEADS)
D_PROJ = sum(PROJ_SIZES)
PROJ_CUTS = [int(c) for c in np.cumsum(PROJ_SIZES)[:-1]]
EPS = 1e-6

kernel_name = "hymba_conv_gdn_stream_step"


def _rmsnorm(x, w):
    xf = x.astype(jnp.float32)
    y = xf * lax.rsqrt(jnp.mean(xf * xf, axis=-1, keepdims=True) + EPS) * w.astype(jnp.float32)
    return y.astype(x.dtype)


def _l2norm(x):
    return x * lax.rsqrt(jnp.sum(x * x, axis=-1, keepdims=True) + EPS)


def _causal_conv(u_full, w):
    width = w.shape[0]
    l = u_full.shape[1] - width + 1
    out = u_full[:, 0:l] * w[0]
    for j in range(1, width):
        out = out + u_full[:, j:j + l] * w[j]
    return out


def _gdn_chunked(q, k, v, g, beta, s0, chunk):
    b, l, h, _ = q.shape
    n = l // chunk

    def blk(t):
        t = t.reshape((b, n, chunk, h) + t.shape[3:])
        return jnp.moveaxis(t, (1, 3), (0, 2))

    qc, kc, vc, bc = blk(q), blk(k), blk(v), blk(beta)
    gc = jnp.cumsum(blk(g), axis=-1)
    idx = jnp.arange(chunk)
    causal = idx[:, None] >= idx[None, :]
    strict = idx[:, None] > idx[None, :]
    decay = jnp.exp(jnp.where(causal, gc[..., :, None] - gc[..., None, :], -jnp.inf))
    kk = jnp.einsum('nbhid,nbhjd->nbhij', kc * bc[..., None], kc) * decay
    eye = jnp.eye(chunk, dtype=kk.dtype)
    a_mat = jnp.where(strict, kk, 0.0) + eye
    t_mat = lax.linalg.triangular_solve(a_mat, jnp.broadcast_to(eye, a_mat.shape),
                                        left_side=True, lower=True, unit_diagonal=True)
    u = jnp.einsum('nbhij,nbhjd->nbhid', t_mat, vc * bc[..., None])
    w = jnp.einsum('nbhij,nbhjd->nbhid', t_mat, kc * (bc * jnp.exp(gc))[..., None])
    qk = jnp.einsum('nbhid,nbhjd->nbhij', qc, kc) * decay

    def step(s, inp):
        q_i, k_i, u_i, w_i, g_i, qk_i = inp
        v_new = u_i - jnp.einsum('bhcd,bhde->bhce', w_i, s)
        o = (jnp.einsum('bhcd,bhde->bhce', q_i * jnp.exp(g_i)[..., None], s)
             + jnp.einsum('bhij,bhje->bhie', qk_i, v_new))
        g_last = g_i[..., -1]
        s = (s * jnp.exp(g_last)[..., None, None]
             + jnp.einsum('bhcd,bhce->bhde', k_i * jnp.exp(g_last[..., None] - g_i)[..., None], v_new))
        return s, o

    s_fin, o = lax.scan(step, s0, (qc, kc, u, w, gc, qk))
    o = jnp.moveaxis(o, (0, 2), (1, 3)).reshape(b, l, h, v.shape[-1])
    return s_fin, o


def _layer(h, conv_a_prev, conv_qkv_prev, s_prev, n_meta,
           norm_w, w_in, conv_a_w, conv_qkv_w, a_log, dt_bias, o_norm_w, w_out):
    dt = h.dtype
    f32 = jnp.float32
    b, l, _ = h.shape
    xn = _rmsnorm(h, norm_w)
    proj = jnp.einsum('bld,dp->blp', xn, w_in)
    a_b, a_c, a_x, a_z, qkv, z_b, b_logit, a_logit = jnp.split(proj, PROJ_CUTS, axis=-1)

    u_full = jnp.concatenate([conv_a_prev.astype(dt), a_c * a_x], axis=1)
    y_a = a_b * _causal_conv(u_full, conv_a_w) * jax.nn.silu(a_z)
    new_conv_a = u_full[:, -(CONV_A_W - 1):]

    qkv_full = jnp.concatenate([conv_qkv_prev.astype(dt), qkv], axis=1)
    new_conv_qkv = qkv_full[:, -(CONV_QKV_W - 1):]
    qkv_c = jax.nn.silu(_causal_conv(qkv_full, conv_qkv_w)).astype(f32)
    q, k, v = jnp.split(qkv_c, 3, axis=-1)
    q = _l2norm(q.reshape(b, l, N_HEADS, HEAD_DIM)) * (HEAD_DIM ** -0.5)
    k = _l2norm(k.reshape(b, l, N_HEADS, HEAD_DIM))
    v = v.reshape(b, l, N_HEADS, HEAD_DIM)
    beta = jax.nn.sigmoid(b_logit.astype(f32))
    g = -jnp.exp(a_log.astype(f32)) * jax.nn.softplus(a_logit.astype(f32) + dt_bias.astype(f32))
    s = s_prev.astype(f32)
    if n_meta > 0:
        s, o_meta = _gdn_chunked(q[:, :n_meta], k[:, :n_meta], v[:, :n_meta],
                                 g[:, :n_meta], beta[:, :n_meta], s, n_meta)
        s, o_rest = _gdn_chunked(q[:, n_meta:], k[:, n_meta:], v[:, n_meta:],
                                 g[:, n_meta:], beta[:, n_meta:], s, CHUNK)
        o = jnp.concatenate([o_meta, o_rest], axis=1)
    else:
        s, o = _gdn_chunked(q, k, v, g, beta, s, min(l, CHUNK))
    o = _rmsnorm(o, o_norm_w) * jax.nn.silu(z_b.astype(f32).reshape(b, l, N_HEADS, HEAD_DIM))
    y_b = o.reshape(b, l, D_DELTA).astype(dt)

    y = jnp.einsum('blm,md->bld', jnp.concatenate([y_a, y_b], axis=-1), w_out)
    return h + y, new_conv_a, new_conv_qkv, s.astype(dt)


def setup_inputs(seed: int = 0) -> dict:
    key = jax.random.key(seed)
    ks = jax.random.split(key, 16)
    f32 = jnp.float32
    nrm = jax.random.normal
    x_prompt = nrm(ks[0], (BATCH, SEQ, D_MODEL), f32)
    x_sample = nrm(ks[1], (DEC_BATCH, DEC_SEQ, D_MODEL), f32)
    state_conv_a = nrm(ks[2], (DEPTH, DEC_BATCH, CONV_A_W - 1, D_CONV_BR), f32)
    state_conv_qkv = nrm(ks[3], (DEPTH, DEC_BATCH, CONV_QKV_W - 1, D_QKV), f32)
    state_delta = 0.1 * nrm(ks[4], (DEPTH, DEC_BATCH, N_HEADS, HEAD_DIM, HEAD_DIM), f32)
    meta_tokens = nrm(ks[5], (N_META, D_MODEL), f32)
    norm_w = 1.0 + 0.02 * nrm(ks[6], (DEPTH, D_MODEL), f32)
    w_in = nrm(ks[7], (DEPTH, D_MODEL, D_PROJ), f32) * (D_MODEL ** -0.5)
    conv_a_w = nrm(ks[8], (DEPTH, CONV_A_W, D_CONV_BR), f32) * (CONV_A_W ** -0.5)
    conv_qkv_w = nrm(ks[9], (DEPTH, CONV_QKV_W, D_QKV), f32) * (CONV_QKV_W ** -0.5)
    a_log = jnp.log(jax.random.uniform(ks[10], (DEPTH, N_HEADS), f32, 1.0, 16.0))
    dt0 = jnp.exp(jax.random.uniform(ks[11], (DEPTH, N_HEADS), f32, math.log(1e-3), math.log(1e-1)))
    dt_bias = dt0 + jnp.log(-jnp.expm1(-dt0))
    o_norm_w = 1.0 + 0.02 * nrm(ks[12], (DEPTH, HEAD_DIM), f32)
    w_out = nrm(ks[13], (DEPTH, D_MIX, D_MODEL), f32) * (D_MIX ** -0.5)
    final_norm_w = 1.0 + 0.02 * nrm(ks[14], (D_MODEL,), f32)
    return {"x_prompt": x_prompt, "x_sample": x_sample,
            "state_conv_a": state_conv_a, "state_conv_qkv": state_conv_qkv, "state_delta": state_delta,
            "meta_tokens": meta_tokens, "norm_w": norm_w, "w_in": w_in, "conv_a_w": conv_a_w,
            "conv_qkv_w": conv_qkv_w, "a_log": a_log, "dt_bias": dt_bias, "o_norm_w": o_norm_w,
            "w_out": w_out, "final_norm_w": final_norm_w}


def reference(x_prompt, x_sample, state_conv_a, state_conv_qkv, state_delta,
              meta_tokens, norm_w, w_in, conv_a_w, conv_qkv_w, a_log, dt_bias, o_norm_w,
              w_out, final_norm_w):
    dt = x_prompt.dtype
    b = x_prompt.shape[0]
    meta = jnp.broadcast_to(meta_tokens.astype(dt)[None], (b, N_META, D_MODEL))
    hp = jnp.concatenate([meta, x_prompt], axis=1)
    hs = x_sample
    zero_a = jnp.zeros((b, CONV_A_W - 1, D_CONV_BR), dt)
    zero_qkv = jnp.zeros((b, CONV_QKV_W - 1, D_QKV), dt)
    zero_s = jnp.zeros((b, N_HEADS, HEAD_DIM, HEAD_DIM), jnp.float32)
    p_a, p_qkv, p_s, s_a, s_qkv, s_s = [], [], [], [], [], []
    for layer in range(DEPTH):
        wts = (norm_w[layer], w_in[layer], conv_a_w[layer], conv_qkv_w[layer],
               a_log[layer], dt_bias[layer], o_norm_w[layer], w_out[layer])
        hp, ca, cq, st = _layer(hp, zero_a, zero_qkv, zero_s, N_META, *wts)
        p_a.append(ca); p_qkv.append(cq); p_s.append(st)
        hs, ca, cq, st = _layer(hs, state_conv_a[layer], state_conv_qkv[layer], state_delta[layer], 0, *wts)
        s_a.append(ca); s_qkv.append(cq); s_s.append(st)
    y_prompt = _rmsnorm(hp[:, N_META:], final_norm_w)
    y_sample = _rmsnorm(hs, final_norm_w)
    return (y_prompt, y_sample, jnp.stack(p_a), jnp.stack(p_qkv), jnp.stack(p_s),
            jnp.stack(s_a), jnp.stack(s_qkv), jnp.stack(s_s))
```

```python
import functools

import jax
import jax.numpy as jnp
from jax import lax
from jax.experimental import pallas as pl
from jax.experimental.pallas import tpu as pltpu

D_MODEL = 2048
N_META = 16
CHUNK = 64
D_A = 1024
N_HEADS = 8
HEAD_DIM = 128
D_DELTA = N_HEADS * HEAD_DIM
D_QKV = 3 * D_DELTA
D_MAIN = 4 * D_A + D_QKV + D_DELTA
D_MIX = D_A + D_DELTA
GATE_LANES = 128
CONV_A_W = 3
CONV_QKV_W = 4
EPS = 1e-6

F32 = jnp.float32
BF16 = jnp.bfloat16

OFF_B, OFF_C, OFF_X, OFF_ZA = 0, D_A, 2 * D_A, 3 * D_A
OFF_QKV = 4 * D_A
OFF_ZB = OFF_QKV + D_QKV

VMEM_LIMIT = 56 * 1024 * 1024


def _silu(x):
    return x / (1.0 + jnp.exp(-x))


def _softplus(x):
    return jnp.maximum(x, 0.0) + jnp.log1p(jnp.exp(-jnp.abs(x)))


def _dot(a, b):
    return jnp.dot(a, b, preferred_element_type=F32)


def _dot_nt(a, b):
    return lax.dot_general(a, b, (((1,), (1,)), ((), ())), preferred_element_type=F32)


def _dot_tn(a, b):
    return lax.dot_general(a, b, (((0,), (0,)), ((), ())), preferred_element_type=F32)


def _dot_hi(a, b):
    return jnp.dot(a, b, preferred_element_type=F32, precision=lax.Precision.HIGHEST)


def _inproj_kernel(h_ref, nw_ref, w_ref, wg_ref, proj_ref, gate_ref, xn_ref):
    @pl.when(pl.program_id(1) == 0)
    def _norm():
        x = h_ref[...]
        ms = jnp.mean(x * x, axis=-1, keepdims=True)
        xn = (x * lax.rsqrt(ms + EPS) * nw_ref[...]).astype(BF16)
        xn_ref[...] = xn
        gate_ref[...] = _dot(xn, wg_ref[...])

    proj_ref[...] = _dot(xn_ref[...], w_ref[...]).astype(proj_ref.dtype)


def _inproj(h, norm_w, w_main, w_gate, *, tm, bn):
    n = h.shape[0]
    assert n % tm == 0 and D_MAIN % bn == 0
    return pl.pallas_call(
        _inproj_kernel,
        grid=(n // tm, D_MAIN // bn),
        in_specs=[
            pl.BlockSpec((tm, D_MODEL), lambda i, j: (i, 0)),
            pl.BlockSpec((1, D_MODEL), lambda i, j: (0, 0)),
            pl.BlockSpec((D_MODEL, bn), lambda i, j: (0, j)),
            pl.BlockSpec((D_MODEL, GATE_LANES), lambda i, j: (0, 0)),
        ],
        out_specs=[
            pl.BlockSpec((tm, bn), lambda i, j: (i, j)),
            pl.BlockSpec((tm, GATE_LANES), lambda i, j: (i, 0)),
        ],
        out_shape=[
            jax.ShapeDtypeStruct((n, D_MAIN), BF16),
            jax.ShapeDtypeStruct((n, GATE_LANES), F32),
        ],
        scratch_shapes=[pltpu.VMEM((tm, D_MODEL), BF16)],
        compiler_params=pltpu.CompilerParams(
            dimension_semantics=("arbitrary", "arbitrary"), vmem_limit_bytes=VMEM_LIMIT),
        name="inproj",
    )(h, norm_w, w_main, w_gate)


def _outproj_kernel(y_ref, h_ref, w_ref, o_ref):
    o_ref[...] = h_ref[...] + _dot(y_ref[...], w_ref[...])


def _outproj_final_kernel(y_ref, h_ref, w_ref, fw_ref, o_ref):
    hn = h_ref[...] + _dot(y_ref[...], w_ref[...])
    ms = jnp.mean(hn * hn, axis=-1, keepdims=True)
    o_ref[...] = hn * lax.rsqrt(ms + EPS) * fw_ref[...]


def _outproj(y, h, w_out, final_w, *, tm):
    n = h.shape[0]
    assert n % tm == 0
    row = pl.BlockSpec((tm, D_MODEL), lambda i: (i, 0))
    in_specs = [row, row, pl.BlockSpec((D_MIX, D_MODEL), lambda i: (0, 0))]
    args = [y, h, w_out]
    body = _outproj_kernel
    if final_w is not None:
        in_specs.append(pl.BlockSpec((1, D_MODEL), lambda i: (0, 0)))
        args.append(final_w)
        body = _outproj_final_kernel
    return pl.pallas_call(
        body,
        grid=(n // tm,),
        in_specs=in_specs,
        out_specs=row,
        out_shape=jax.ShapeDtypeStruct((n, D_MODEL), F32),
        compiler_params=pltpu.CompilerParams(
            dimension_semantics=("arbitrary",), vmem_limit_bytes=VMEM_LIMIT),
        name="outproj",
    )(*args)


def _inv_unit_lower(n_mat, eye):
    p = eye - n_mat
    m = _dot_hi(n_mat, n_mat)
    for _ in range(4):
        p = p + _dot_hi(p, m)
        m = _dot_hi(m, m)
    return p + _dot_hi(p, m)


def _mixer_kernel(proj_ref, gate_ref, ca0_ref, cq0_ref, s0_ref, caw_ref, cqw_ref, gp_ref, onw_ref,
                  y_ref, ca1_ref, cq1_ref, s1_ref, ua_buf, qkv_buf, s_scr, *, n_chunks):
    t = pl.program_id(1)
    a_lo = 8 - (CONV_A_W - 1)
    q_lo = 8 - (CONV_QKV_W - 1)

    @pl.when(t == 0)
    def _init():
        ua_buf[a_lo:8, :] = ca0_ref[...]
        qkv_buf[q_lo:8, :] = cq0_ref[...]
        s_scr[...] = s0_ref[...]

    row = lax.broadcasted_iota(jnp.int32, (CHUNK, CHUNK), 0)
    col = lax.broadcasted_iota(jnp.int32, (CHUNK, CHUNK), 1)
    causal = row >= col
    strict = row > col
    tril = causal.astype(F32)
    eye = (row == col).astype(F32)

    def conv_qkv(c0):
        acc = cqw_ref[0:1, c0:c0 + HEAD_DIM] * qkv_buf[q_lo:q_lo + CHUNK, c0:c0 + HEAD_DIM]
        for j in range(1, CONV_QKV_W):
            acc = acc + cqw_ref[j:j + 1, c0:c0 + HEAD_DIM] * qkv_buf[q_lo + j:q_lo + j + CHUNK, c0:c0 + HEAD_DIM]
        return _silu(acc)

    def l2norm(x):
        return x * lax.rsqrt(jnp.sum(x * x, axis=-1, keepdims=True) + EPS)

    def chunk_body(ci, carry):
        r0 = pl.multiple_of(ci * CHUNK, CHUNK)
        rows = pl.ds(r0, CHUNK)

        a_b = proj_ref[rows, OFF_B:OFF_B + D_A].astype(F32)
        a_c = proj_ref[rows, OFF_C:OFF_C + D_A].astype(F32)
        a_x = proj_ref[rows, OFF_X:OFF_X + D_A].astype(F32)
        a_z = proj_ref[rows, OFF_ZA:OFF_ZA + D_A].astype(F32)
        u = a_c * a_x
        ua_buf[8:8 + CHUNK, :] = u
        conv = caw_ref[CONV_A_W - 1:CONV_A_W, :] * u
        for j in range(CONV_A_W - 1):
            conv = conv + caw_ref[j:j + 1, :] * ua_buf[a_lo + j:a_lo + j + CHUNK, :]
        y_ref[rows, 0:D_A] = (a_b * conv * _silu(a_z)).astype(y_ref.dtype)
        ua_buf[a_lo:8, :] = ua_buf[a_lo + CHUNK:8 + CHUNK, :]

        qkv_buf[8:8 + CHUNK, :] = proj_ref[rows, OFF_QKV:OFF_QKV + D_QKV].astype(F32)

        gl = gate_ref[rows, :]
        beta_t = 1.0 / (1.0 + jnp.exp(-gl))
        g_t = -jnp.exp(gp_ref[0:1, :]) * _softplus(gl + gp_ref[1:2, :])
        gc = _dot_hi(tril, g_t)
        gc_t = gc.T

        for h in range(N_HEADS):
            beta_c = beta_t[:, h:h + 1]
            gc_c = gc[:, N_HEADS + h:N_HEADS + h + 1]
            gc_r = gc_t[N_HEADS + h:N_HEADS + h + 1, :]
            g_last = gc[CHUNK - 1:CHUNK, N_HEADS + h:N_HEADS + h + 1]
            decay = jnp.where(causal, jnp.exp(gc_c - gc_r), 0.0)
            eg_c = jnp.exp(gc_c)

            q = l2norm(conv_qkv(h * HEAD_DIM)) * (HEAD_DIM ** -0.5)
            k = l2norm(conv_qkv(D_DELTA + h * HEAD_DIM))
            v = conv_qkv(2 * D_DELTA + h * HEAD_DIM)

            kb = k * beta_c
            k16 = k.astype(BF16)
            kk = _dot_nt(kb.astype(BF16), k16) * decay
            t_mat = _inv_unit_lower(jnp.where(strict, kk, 0.0), eye)
            qk = _dot_nt(q.astype(BF16), k16) * decay

            rhs = jnp.concatenate([v * beta_c, kb * eg_c], axis=1)
            uw = _dot(t_mat.astype(BF16), rhs.astype(BF16))
            u_i = uw[:, 0:HEAD_DIM]
            w_i = uw[:, HEAD_DIM:2 * HEAD_DIM]

            s = s_scr[h]
            s16 = s.astype(BF16)
            v_new = u_i - _dot(w_i.astype(BF16), s16)
            vn16 = v_new.astype(BF16)
            o = _dot((q * eg_c).astype(BF16), s16) + _dot(qk.astype(BF16), vn16)
            kd = k * jnp.exp(g_last - gc_c)
            s_scr[h] = s * jnp.exp(g_last) + _dot_tn(kd.astype(BF16), vn16)

            ms = jnp.mean(o * o, axis=-1, keepdims=True)
            on = o * lax.rsqrt(ms + EPS) * onw_ref[...]
            zb = proj_ref[rows, OFF_ZB + h * HEAD_DIM:OFF_ZB + (h + 1) * HEAD_DIM].astype(F32)
            c0 = D_A + h * HEAD_DIM
            y_ref[rows, c0:c0 + HEAD_DIM] = (on * _silu(zb)).astype(y_ref.dtype)

        qkv_buf[q_lo:8, :] = qkv_buf[q_lo + CHUNK:8 + CHUNK, :]
        return carry

    lax.fori_loop(0, n_chunks, chunk_body, 0)

    @pl.when(t == pl.num_programs(1) - 1)
    def _fin():
        ca1_ref[...] = ua_buf[a_lo:8, :]
        cq1_ref[...] = qkv_buf[q_lo:8, :]
        s1_ref[...] = s_scr[...]


def _mixer(proj, gates, ca0, cq0, s0, conv_a_w, conv_qkv_w, gate_par, o_norm_w, *, tile):
    b, l, _ = proj.shape
    assert l % tile == 0 and tile % CHUNK == 0
    n_chunks = tile // CHUNK

    def seq(width):
        return pl.BlockSpec((None, tile, width), lambda i, t: (i, t, 0))

    def per_batch(shape):
        return pl.BlockSpec((None,) + shape, lambda i, t: (i,) + (0,) * len(shape))

    def const(shape):
        return pl.BlockSpec(shape, lambda i, t: (0,) * len(shape))

    st_a = (CONV_A_W - 1, D_A)
    st_q = (CONV_QKV_W - 1, D_QKV)
    st_s = (N_HEADS, HEAD_DIM, HEAD_DIM)
    return pl.pallas_call(
        functools.partial(_mixer_kernel, n_chunks=n_chunks),
        grid=(b, l // tile),
        in_specs=[
            seq(D_MAIN), seq(GATE_LANES),
            per_batch(st_a), per_batch(st_q), per_batch(st_s),
            const((CONV_A_W, D_A)), const((CONV_QKV_W, D_QKV)),
            const((2, GATE_LANES)), const((1, HEAD_DIM)),
        ],
        out_specs=[seq(D_MIX), per_batch(st_a), per_batch(st_q), per_batch(st_s)],
        out_shape=[
            jax.ShapeDtypeStruct((b, l, D_MIX), BF16),
            jax.ShapeDtypeStruct((b,) + st_a, F32),
            jax.ShapeDtypeStruct((b,) + st_q, F32),
            jax.ShapeDtypeStruct((b,) + st_s, F32),
        ],
        scratch_shapes=[
            pltpu.VMEM((8 + CHUNK, D_A), F32),
            pltpu.VMEM((8 + CHUNK, D_QKV), F32),
            pltpu.VMEM(st_s, F32),
        ],
        compiler_params=pltpu.CompilerParams(
            dimension_semantics=("arbitrary", "arbitrary"), vmem_limit_bytes=VMEM_LIMIT),
        name="mixer",
    )(proj, gates, ca0, cq0, s0, conv_a_w, conv_qkv_w, gate_par, o_norm_w)


def _pick_tile(n, target):
    t = min(n, target)
    while n % t:
        t //= 2
    return t


def kernel(x_prompt, x_sample, state_conv_a, state_conv_qkv, state_delta, meta_tokens, norm_w, w_in,
           conv_a_w, conv_qkv_w, a_log, dt_bias, o_norm_w, w_out, final_norm_w):
    b, seq, d = x_prompt.shape
    bs, dec_seq, _ = x_sample.shape
    depth = norm_w.shape[0]
    assert d == D_MODEL and dec_seq == CHUNK and seq % CHUNK == 0

    meta_blk = jnp.concatenate([jnp.zeros((CHUNK - N_META, d), F32), meta_tokens.astype(F32)], axis=0)
    h_small = jnp.concatenate([jnp.broadcast_to(meta_blk[None], (b, CHUNK, d)), x_sample], axis=0)
    h_small = h_small.reshape((b + bs) * CHUNK, d)
    h_main = x_prompt.reshape(b * seq, d)
    n_small, n_main = h_small.shape[0], h_main.shape[0]

    p_a, p_qkv, p_s, s_a, s_qkv, s_s = [], [], [], [], [], []
    for layer in range(depth):
        w_main = w_in[layer, :, :D_MAIN].astype(BF16)
        w_gate = jnp.pad(w_in[layer, :, D_MAIN:], ((0, 0), (0, GATE_LANES - 2 * N_HEADS))).astype(BF16)
        w_o = w_out[layer].astype(BF16)
        nw = norm_w[layer].reshape(1, d)
        gate_par = jnp.zeros((2, GATE_LANES), F32)
        gate_par = gate_par.at[0, N_HEADS:2 * N_HEADS].set(a_log[layer])
        gate_par = gate_par.at[1, N_HEADS:2 * N_HEADS].set(dt_bias[layer])
        onw = o_norm_w[layer].reshape(1, HEAD_DIM)
        fw = final_norm_w.reshape(1, d) if layer == depth - 1 else None
        mix = functools.partial(_mixer, conv_a_w=conv_a_w[layer], conv_qkv_w=conv_qkv_w[layer],
                                gate_par=gate_par, o_norm_w=onw)

        ca0 = jnp.concatenate([jnp.zeros((b,) + state_conv_a.shape[2:], F32), state_conv_a[layer]], axis=0)
        cq0 = jnp.concatenate([jnp.zeros((b,) + state_conv_qkv.shape[2:], F32), state_conv_qkv[layer]], axis=0)
        s0 = jnp.concatenate([jnp.zeros((b,) + state_delta.shape[2:], F32), state_delta[layer]], axis=0)
        proj, gates = _inproj(h_small, nw, w_main, w_gate, tm=_pick_tile(n_small, 1024), bn=1024)
        y, ca1, cq1, s1 = mix(proj.reshape(b + bs, CHUNK, D_MAIN), gates.reshape(b + bs, CHUNK, GATE_LANES),
                              ca0, cq0, s0, tile=CHUNK)
        h_small = _outproj(y.reshape(n_small, D_MIX), h_small, w_o, fw, tm=_pick_tile(n_small, 512))
        s_a.append(ca1[b:]); s_qkv.append(cq1[b:]); s_s.append(s1[b:])

        proj, gates = _inproj(h_main, nw, w_main, w_gate, tm=_pick_tile(n_main, 1024), bn=1024)
        y, ca2, cq2, s2 = mix(proj.reshape(b, seq, D_MAIN), gates.reshape(b, seq, GATE_LANES),
                              ca1[:b], cq1[:b], s1[:b], tile=_pick_tile(seq, 512))
        h_main = _outproj(y.reshape(n_main, D_MIX), h_main, w_o, fw, tm=_pick_tile(n_main, 512))
        p_a.append(ca2); p_qkv.append(cq2); p_s.append(s2)

    y_prompt = h_main.reshape(b, seq, d)
    y_sample = h_small.reshape(b + bs, CHUNK, d)[b:]
    return (y_prompt, y_sample, jnp.stack(p_a), jnp.stack(p_qkv), jnp.stack(p_s),
            jnp.stack(s_a), jnp.stack(s_qkv), jnp.stack(s_s))
```

```python
import functools

import jax
import jax.numpy as jnp
from jax import lax
from jax.experimental import pallas as pl
from jax.experimental.pallas import tpu as pltpu

D_MODEL = 2048
N_META = 16
CHUNK = 64
D_A = 1024
N_HEADS = 8
HEAD_DIM = 128
D_DELTA = N_HEADS * HEAD_DIM
D_QKV = 3 * D_DELTA
D_MAIN = 4 * D_A + D_QKV + D_DELTA
D_MIX = D_A + D_DELTA
GATE_LANES = 128
CONV_A_W = 3
CONV_QKV_W = 4
EPS = 1e-6

F32 = jnp.float32
BF16 = jnp.bfloat16

OFF_B, OFF_C, OFF_X, OFF_ZA = 0, D_A, 2 * D_A, 3 * D_A
OFF_QKV = 4 * D_A
OFF_ZB = OFF_QKV + D_QKV

VMEM_LIMIT = 56 * 1024 * 1024


def _sigmoid(x):
    return 0.5 + 0.5 * jnp.tanh(0.5 * x)


def _silu(x):
    hx = 0.5 * x
    return hx + hx * jnp.tanh(hx)


def _softplus(x):
    return jnp.maximum(x, 0.0) + jnp.log1p(jnp.exp(-jnp.abs(x)))


def _dot(a, b):
    return jnp.dot(a, b, preferred_element_type=F32)


def _dot_hi(a, b):
    return jnp.dot(a, b, preferred_element_type=F32, precision=lax.Precision.HIGHEST)


def _inproj_kernel(h_ref, nw_ref, w_ref, wg_ref, proj_ref, gate_ref, xn_ref):
    @pl.when(pl.program_id(1) == 0)
    def _norm():
        x = h_ref[...]
        ms = jnp.mean(x * x, axis=-1, keepdims=True)
        xn = (x * lax.rsqrt(ms + EPS) * nw_ref[...]).astype(BF16)
        xn_ref[...] = xn
        gate_ref[...] = _dot(xn, wg_ref[...])

    proj_ref[...] = _dot(xn_ref[...], w_ref[...]).astype(proj_ref.dtype)


def _inproj(h, norm_w, w_main, w_gate, *, tm, bn):
    n = h.shape[0]
    assert n % tm == 0 and D_MAIN % bn == 0
    return pl.pallas_call(
        _inproj_kernel,
        grid=(n // tm, D_MAIN // bn),
        in_specs=[
            pl.BlockSpec((tm, D_MODEL), lambda i, j: (i, 0)),
            pl.BlockSpec((1, D_MODEL), lambda i, j: (0, 0)),
            pl.BlockSpec((D_MODEL, bn), lambda i, j: (0, j)),
            pl.BlockSpec((D_MODEL, GATE_LANES), lambda i, j: (0, 0)),
        ],
        out_specs=[
            pl.BlockSpec((tm, bn), lambda i, j: (i, j)),
            pl.BlockSpec((tm, GATE_LANES), lambda i, j: (i, 0)),
        ],
        out_shape=[
            jax.ShapeDtypeStruct((n, D_MAIN), BF16),
            jax.ShapeDtypeStruct((n, GATE_LANES), F32),
        ],
        scratch_shapes=[pltpu.VMEM((tm, D_MODEL), BF16)],
        compiler_params=pltpu.CompilerParams(
            dimension_semantics=("arbitrary", "arbitrary"), vmem_limit_bytes=VMEM_LIMIT),
        name="inproj",
    )(h, norm_w, w_main, w_gate)


def _outproj_kernel(y_ref, h_ref, w_ref, o_ref):
    o_ref[...] = h_ref[...] + _dot(y_ref[...], w_ref[...])


def _outproj_final_kernel(y_ref, h_ref, w_ref, fw_ref, o_ref):
    hn = h_ref[...] + _dot(y_ref[...], w_ref[...])
    ms = jnp.mean(hn * hn, axis=-1, keepdims=True)
    o_ref[...] = hn * lax.rsqrt(ms + EPS) * fw_ref[...]


def _outproj(y, h, w_out, final_w, *, tm):
    n = h.shape[0]
    assert n % tm == 0
    row = pl.BlockSpec((tm, D_MODEL), lambda i: (i, 0))
    in_specs = [row, row, pl.BlockSpec((D_MIX, D_MODEL), lambda i: (0, 0))]
    args = [y, h, w_out]
    body = _outproj_kernel
    if final_w is not None:
        in_specs.append(pl.BlockSpec((1, D_MODEL), lambda i: (0, 0)))
        args.append(final_w)
        body = _outproj_final_kernel
    return pl.pallas_call(
        body,
        grid=(n // tm,),
        in_specs=in_specs,
        out_specs=row,
        out_shape=jax.ShapeDtypeStruct((n, D_MODEL), F32),
        compiler_params=pltpu.CompilerParams(
            dimension_semantics=("arbitrary",), vmem_limit_bytes=VMEM_LIMIT),
        name="outproj",
    )(*args)


def _block_mask(row, col, size):
    shift = size.bit_length() - 1
    return (row >> shift) == (col >> shift)


def _inv_unit_lower(n_mats, row, col):
    eye = (row == col).astype(F32)
    in8 = _block_mask(row, col, 8)
    nd = [jnp.where(in8, n, 0.0) for n in n_mats]
    nd16 = [x.astype(BF16) for x in nd]
    m2_16 = [_dot(x, x).astype(BF16) for x in nd16]
    p = [eye - x for x in nd]
    p = [x + _dot(x.astype(BF16), m) for x, m in zip(p, m2_16)]
    m4_16 = [_dot(m, m).astype(BF16) for m in m2_16]
    t_mats = [x + _dot(x.astype(BF16), m) for x, m in zip(p, m4_16)]
    size = 8
    while size < CHUNK:
        off = jnp.logical_and(_block_mask(row, col, 2 * size), jnp.logical_not(_block_mask(row, col, size)))
        n_off16 = [jnp.where(off, n, 0.0).astype(BF16) for n in n_mats]
        t16 = [x.astype(BF16) for x in t_mats]
        a16 = [_dot(n, x).astype(BF16) for n, x in zip(n_off16, t16)]
        t_mats = [x - _dot(x16, a) for x, x16, a in zip(t_mats, t16, a16)]
        size *= 2
    return t_mats


def _mixer_kernel(proj_ref, gate_ref, ca0_ref, cq0_ref, s0_ref, caw_ref, cqw_ref, gp_ref, onw_ref,
                  y_ref, ca1_ref, cq1_ref, s1_ref,
                  ua_buf, qkv_buf, s_scr, u_scr, wq_scr, qk_scr, kdt_scr, gl_scr, *, n_chunks):
    t = pl.program_id(1)
    a_lo = 8 - (CONV_A_W - 1)
    q_lo = 8 - (CONV_QKV_W - 1)

    @pl.when(t == 0)
    def _init():
        ua_buf[a_lo:8, :] = ca0_ref[...]
        qkv_buf[q_lo:8, :] = cq0_ref[...]
        s_scr[...] = s0_ref[...]

    row = lax.broadcasted_iota(jnp.int32, (CHUNK, CHUNK), 0)
    col = lax.broadcasted_iota(jnp.int32, (CHUNK, CHUNK), 1)
    causal = row >= col
    strict = row > col
    tril = causal.astype(F32)

    def conv_qkv(c0):
        acc = cqw_ref[0:1, c0:c0 + HEAD_DIM] * qkv_buf[q_lo:q_lo + CHUNK, c0:c0 + HEAD_DIM]
        for j in range(1, CONV_QKV_W):
            acc = acc + cqw_ref[j:j + 1, c0:c0 + HEAD_DIM] * qkv_buf[q_lo + j:q_lo + j + CHUNK, c0:c0 + HEAD_DIM]
        return _silu(acc)

    def l2norm(x):
        return x * lax.rsqrt(jnp.sum(x * x, axis=-1, keepdims=True) + EPS)

    def chunk_body(ci, carry):
        r0 = pl.multiple_of(ci * CHUNK, CHUNK)
        rows = pl.ds(r0, CHUNK)

        a_b = proj_ref[rows, OFF_B:OFF_B + D_A].astype(F32)
        a_c = proj_ref[rows, OFF_C:OFF_C + D_A].astype(F32)
        a_x = proj_ref[rows, OFF_X:OFF_X + D_A].astype(F32)
        a_z = proj_ref[rows, OFF_ZA:OFF_ZA + D_A].astype(F32)
        u = a_c * a_x
        ua_buf[8:8 + CHUNK, :] = u
        conv = caw_ref[CONV_A_W - 1:CONV_A_W, :] * u
        for j in range(CONV_A_W - 1):
            conv = conv + caw_ref[j:j + 1, :] * ua_buf[a_lo + j:a_lo + j + CHUNK, :]
        y_ref[rows, 0:D_A] = (a_b * conv * _silu(a_z)).astype(y_ref.dtype)
        ua_buf[a_lo:8, :] = ua_buf[a_lo + CHUNK:8 + CHUNK, :]

        qkv_buf[8:8 + CHUNK, :] = proj_ref[rows, OFF_QKV:OFF_QKV + D_QKV].astype(F32)

        gl = gate_ref[rows, :]
        beta_t = _sigmoid(gl)
        g_t = -jnp.exp(gp_ref[0:1, :]) * _softplus(gl + gp_ref[1:2, :])
        gc = _dot_hi(tril, g_t)
        gc_t = gc.T

        gl_scr[pl.ds(ci, 1), :] = gc[CHUNK - 1:CHUNK, :]

        heads = range(N_HEADS)
        q = [l2norm(conv_qkv(h * HEAD_DIM)) * (HEAD_DIM ** -0.5) for h in heads]
        k = [l2norm(conv_qkv(D_DELTA + h * HEAD_DIM)) for h in heads]
        v = [conv_qkv(2 * D_DELTA + h * HEAD_DIM) for h in heads]
        qkv_buf[q_lo:8, :] = qkv_buf[q_lo + CHUNK:8 + CHUNK, :]

        beta_c = [beta_t[:, h:h + 1] for h in heads]
        gc_c = [gc[:, N_HEADS + h:N_HEADS + h + 1] for h in heads]
        gc_r = [gc_t[N_HEADS + h:N_HEADS + h + 1, :] for h in heads]
        g_last = [gc[CHUNK - 1:CHUNK, N_HEADS + h:N_HEADS + h + 1] for h in heads]
        decay = [jnp.where(causal, jnp.exp(gc_c[h] - gc_r[h]), 0.0) for h in heads]
        eg_c = [jnp.exp(gc_c[h]) for h in heads]

        kb = [k[h] * beta_c[h] for h in heads]
        k_t = [k[h].T for h in heads]
        kq = [_dot(jnp.concatenate([kb[h], q[h]], axis=0).astype(BF16), k_t[h].astype(BF16))
              for h in heads]
        for h in heads:
            qk_scr[ci, h] = (kq[h][CHUNK:2 * CHUNK] * decay[h]).astype(BF16)
            kdt_scr[ci, h] = (k_t[h] * jnp.exp(g_last[h] - gc_r[h])).astype(BF16)
        t_mat = _inv_unit_lower([jnp.where(strict, kq[h][0:CHUNK] * decay[h], 0.0) for h in heads], row, col)

        uw = [_dot(t_mat[h].astype(BF16),
                   jnp.concatenate([v[h] * beta_c[h], kb[h] * eg_c[h]], axis=1).astype(BF16))
              for h in heads]
        for h in heads:
            u_scr[ci, :, h * HEAD_DIM:(h + 1) * HEAD_DIM] = uw[h][:, 0:HEAD_DIM]
            wq_scr[ci, h, 0:CHUNK, :] = uw[h][:, HEAD_DIM:2 * HEAD_DIM].astype(BF16)
            wq_scr[ci, h, CHUNK:2 * CHUNK, :] = (q[h] * eg_c[h]).astype(BF16)
        return carry

    def state_body(ci, carry):
        r0 = pl.multiple_of(ci * CHUNK, CHUNK)
        rows = pl.ds(r0, CHUNK)
        heads = range(N_HEADS)
        eg_last = jnp.exp(gl_scr[pl.ds(ci, 1), :])
        s = [s_scr[h] for h in heads]
        ws = [_dot(wq_scr[ci, h], s[h].astype(BF16)) for h in heads]
        vn16 = [(u_scr[ci, :, h * HEAD_DIM:(h + 1) * HEAD_DIM] - ws[h][0:CHUNK]).astype(BF16) for h in heads]
        for h in heads:
            s_scr[h] = (s[h] * eg_last[:, N_HEADS + h:N_HEADS + h + 1]
                        + _dot(kdt_scr[ci, h], vn16[h]))
        o = [ws[h][CHUNK:2 * CHUNK] + _dot(qk_scr[ci, h], vn16[h]) for h in heads]
        for h in heads:
            ms = jnp.mean(o[h] * o[h], axis=-1, keepdims=True)
            on = o[h] * lax.rsqrt(ms + EPS) * onw_ref[...]
            zb = proj_ref[rows, OFF_ZB + h * HEAD_DIM:OFF_ZB + (h + 1) * HEAD_DIM].astype(F32)
            c0 = D_A + h * HEAD_DIM
            y_ref[rows, c0:c0 + HEAD_DIM] = (on * _silu(zb)).astype(y_ref.dtype)
        return carry

    lax.fori_loop(0, n_chunks, chunk_body, 0)
    lax.fori_loop(0, n_chunks, state_body, 0)

    @pl.when(t == pl.num_programs(1) - 1)
    def _fin():
        ca1_ref[...] = ua_buf[a_lo:8, :]
        cq1_ref[...] = qkv_buf[q_lo:8, :]
        s1_ref[...] = s_scr[...]


def _mixer(proj, gates, ca0, cq0, s0, conv_a_w, conv_qkv_w, gate_par, o_norm_w, *, tile):
    b, l, _ = proj.shape
    assert l % tile == 0 and tile % CHUNK == 0
    n_chunks = tile // CHUNK

    def seq(width):
        return pl.BlockSpec((None, tile, width), lambda i, t: (i, t, 0))

    def per_batch(shape):
        return pl.BlockSpec((None,) + shape, lambda i, t: (i,) + (0,) * len(shape))

    def const(shape):
        return pl.BlockSpec(shape, lambda i, t: (0,) * len(shape))

    st_a = (CONV_A_W - 1, D_A)
    st_q = (CONV_QKV_W - 1, D_QKV)
    st_s = (N_HEADS, HEAD_DIM, HEAD_DIM)
    return pl.pallas_call(
        functools.partial(_mixer_kernel, n_chunks=n_chunks),
        grid=(b, l // tile),
        in_specs=[
            seq(D_MAIN), seq(GATE_LANES),
            per_batch(st_a), per_batch(st_q), per_batch(st_s),
            const((CONV_A_W, D_A)), const((CONV_QKV_W, D_QKV)),
            const((2, GATE_LANES)), const((1, HEAD_DIM)),
        ],
        out_specs=[seq(D_MIX), per_batch(st_a), per_batch(st_q), per_batch(st_s)],
        out_shape=[
            jax.ShapeDtypeStruct((b, l, D_MIX), BF16),
            jax.ShapeDtypeStruct((b,) + st_a, F32),
            jax.ShapeDtypeStruct((b,) + st_q, F32),
            jax.ShapeDtypeStruct((b,) + st_s, F32),
        ],
        scratch_shapes=[
            pltpu.VMEM((8 + CHUNK, D_A), F32),
            pltpu.VMEM((8 + CHUNK, D_QKV), F32),
            pltpu.VMEM(st_s, F32),
            pltpu.VMEM((n_chunks, CHUNK, D_DELTA), F32),
            pltpu.VMEM((n_chunks, N_HEADS, 2 * CHUNK, HEAD_DIM), BF16),
            pltpu.VMEM((n_chunks, N_HEADS, CHUNK, CHUNK), BF16),
            pltpu.VMEM((n_chunks, N_HEADS, HEAD_DIM, CHUNK), BF16),
            pltpu.VMEM((max(n_chunks, 8), GATE_LANES), F32),
        ],
        compiler_params=pltpu.CompilerParams(
            dimension_semantics=("arbitrary", "arbitrary"), vmem_limit_bytes=VMEM_LIMIT),
        name="mixer",
    )(proj, gates, ca0, cq0, s0, conv_a_w, conv_qkv_w, gate_par, o_norm_w)


def _pick_tile(n, target):
    t = min(n, target)
    while n % t:
        t //= 2
    return t


def kernel(x_prompt, x_sample, state_conv_a, state_conv_qkv, state_delta, meta_tokens, norm_w, w_in,
           conv_a_w, conv_qkv_w, a_log, dt_bias, o_norm_w, w_out, final_norm_w):
    b, seq, d = x_prompt.shape
    bs, dec_seq, _ = x_sample.shape
    depth = norm_w.shape[0]
    assert d == D_MODEL and dec_seq == CHUNK and seq % CHUNK == 0

    meta_blk = jnp.concatenate([jnp.zeros((CHUNK - N_META, d), F32), meta_tokens.astype(F32)], axis=0)
    h_small = jnp.concatenate([jnp.broadcast_to(meta_blk[None], (b, CHUNK, d)), x_sample], axis=0)
    h_small = h_small.reshape((b + bs) * CHUNK, d)
    h_main = x_prompt.reshape(b * seq, d)
    n_small, n_main = h_small.shape[0], h_main.shape[0]

    p_a, p_qkv, p_s, s_a, s_qkv, s_s = [], [], [], [], [], []
    for layer in range(depth):
        w_main = w_in[layer, :, :D_MAIN].astype(BF16)
        w_gate = jnp.pad(w_in[layer, :, D_MAIN:], ((0, 0), (0, GATE_LANES - 2 * N_HEADS))).astype(BF16)
        w_o = w_out[layer].astype(BF16)
        nw = norm_w[layer].reshape(1, d)
        gate_par = jnp.zeros((2, GATE_LANES), F32)
        gate_par = gate_par.at[0, N_HEADS:2 * N_HEADS].set(a_log[layer])
        gate_par = gate_par.at[1, N_HEADS:2 * N_HEADS].set(dt_bias[layer])
        onw = o_norm_w[layer].reshape(1, HEAD_DIM)
        fw = final_norm_w.reshape(1, d) if layer == depth - 1 else None
        mix = functools.partial(_mixer, conv_a_w=conv_a_w[layer], conv_qkv_w=conv_qkv_w[layer],
                                gate_par=gate_par, o_norm_w=onw)

        ca0 = jnp.concatenate([jnp.zeros((b,) + state_conv_a.shape[2:], F32), state_conv_a[layer]], axis=0)
        cq0 = jnp.concatenate([jnp.zeros((b,) + state_conv_qkv.shape[2:], F32), state_conv_qkv[layer]], axis=0)
        s0 = jnp.concatenate([jnp.zeros((b,) + state_delta.shape[2:], F32), state_delta[layer]], axis=0)
        proj, gates = _inproj(h_small, nw, w_main, w_gate, tm=_pick_tile(n_small, 1024), bn=1024)
        y, ca1, cq1, s1 = mix(proj.reshape(b + bs, CHUNK, D_MAIN), gates.reshape(b + bs, CHUNK, GATE_LANES),
                              ca0, cq0, s0, tile=CHUNK)
        h_small = _outproj(y.reshape(n_small, D_MIX), h_small, w_o, fw, tm=_pick_tile(n_small, 512))
        s_a.append(ca1[b:]); s_qkv.append(cq1[b:]); s_s.append(s1[b:])

        proj, gates = _inproj(h_main, nw, w_main, w_gate, tm=_pick_tile(n_main, 1024), bn=1024)
        y, ca2, cq2, s2 = mix(proj.reshape(b, seq, D_MAIN), gates.reshape(b, seq, GATE_LANES),
                              ca1[:b], cq1[:b], s1[:b], tile=_pick_tile(seq, 512))
        h_main = _outproj(y.reshape(n_main, D_MIX), h_main, w_o, fw, tm=_pick_tile(n_main, 512))
        p_a.append(ca2); p_qkv.append(cq2); p_s.append(s2)

    y_prompt = h_main.reshape(b, seq, d)
    y_sample = h_small.reshape(b + bs, CHUNK, d)[b:]
    return (y_prompt, y_sample, jnp.stack(p_a), jnp.stack(p_qkv), jnp.stack(p_s),
            jnp.stack(s_a), jnp.stack(s_qkv), jnp.stack(s_s))
```

```python
import functools

import jax
import jax.numpy as jnp
from jax import lax
from jax.experimental import pallas as pl
from jax.experimental.pallas import tpu as pltpu

D_MODEL = 2048
N_META = 16
CHUNK = 64
D_A = 1024
N_HEADS = 8
HEAD_DIM = 128
D_DELTA = N_HEADS * HEAD_DIM
D_QKV = 3 * D_DELTA
D_MAIN = 4 * D_A + D_QKV + D_DELTA
D_MIX = D_A + D_DELTA
GATE_LANES = 128
CONV_A_W = 3
CONV_QKV_W = 4
EPS = 1e-6

F32 = jnp.float32
BF16 = jnp.bfloat16

BN = 1024
N_BLK = D_MAIN // BN
A_Q = D_A // 4
BLK_Q, BLK_K, BLK_V, BLK_ZB = 4, 5, 6, 7
D_QKVZ = D_QKV + D_DELTA
PIECE = 256
SUB_ROWS = 256
GDN_BATCH = 4
GDN_TILE = 256

VMEM_LIMIT = 56 * 1024 * 1024


def _sigmoid(x):
    return 0.5 + 0.5 * jnp.tanh(0.5 * x)


def _silu(x):
    hx = 0.5 * x
    return hx + hx * jnp.tanh(hx)


def _softplus(x):
    return jnp.maximum(x, 0.0) + jnp.log1p(jnp.exp(-jnp.abs(x)))


def _dot(a, b):
    return jnp.dot(a, b, preferred_element_type=F32)


def _dot_hi(a, b):
    return jnp.dot(a, b, preferred_element_type=F32, precision=lax.Precision.HIGHEST)


def _causal_conv(prev8, x, w_ref, c0, width):
    n = x.shape[1]
    xs = jnp.concatenate([prev8, x], axis=0)
    acc = w_ref[width - 1:width, c0:c0 + n] * xs
    for s in range(1, width):
        acc = acc + w_ref[width - 1 - s:width - s, c0:c0 + n] * pltpu.roll(xs, s, 0)
    return acc[8:]


def _pre_kernel(h_ref, nw_ref, w_ref, wg_ref, caw_ref, cqw_ref, ca0_ref, cq0_ref,
                ya_ref, qkvz_ref, gate_ref, ca1_ref, cq1_ref,
                xn_ref, raw_ref, u_ref, qtail_ref, utail_ref, *, tm, tiles_per_seq):
    i = pl.program_id(0)
    j = pl.program_id(1)
    first = (i % tiles_per_seq) == 0
    sub = min(SUB_ROWS, tm)
    n_sub = tm // sub
    n_piece = BN // PIECE
    rp = sub // n_piece
    a_tail = CONV_A_W - 1
    q_tail = CONV_QKV_W - 1

    @pl.when(j == 0)
    def _norm():
        x = h_ref[...]
        ms = jnp.mean(x * x, axis=-1, keepdims=True)
        xn = (x * lax.rsqrt(ms + EPS) * nw_ref[...]).astype(BF16)
        xn_ref[...] = xn
        gate_ref[...] = _dot(xn, wg_ref[...])

    def mxu_piece(slot, p):
        r, c = divmod(p, n_piece)
        raw_ref[slot, 8 + r * sub:8 + (r + 1) * sub, c * PIECE:(c + 1) * PIECE] = _dot(
            xn_ref[r * sub:(r + 1) * sub, :], w_ref[:, c * PIECE:(c + 1) * PIECE])

    def a_begin(blk, slot):
        @pl.when(first)
        def _():
            u_ref[0:8, :] = jnp.zeros((8, A_Q), F32)
            u_ref[8 - a_tail:8, :] = ca0_ref[...]

        @pl.when(jnp.logical_not(first))
        def _():
            u_ref[0:8, :] = utail_ref[blk]

    def a_piece(blk, slot, p):
        r0 = p * rp
        rb = raw_ref[slot, 8 + r0:8 + r0 + rp, :]
        u = rb[:, A_Q:2 * A_Q] * rb[:, 2 * A_Q:3 * A_Q]
        u_ref[8 + r0:8 + r0 + rp, :] = u
        conv = _causal_conv(u_ref[r0:r0 + 8, :], u, caw_ref, blk * A_Q, CONV_A_W)
        ya_ref[r0:r0 + rp, :] = (rb[:, 0:A_Q] * conv * _silu(rb[:, 3 * A_Q:4 * A_Q])).astype(ya_ref.dtype)

    def a_end(blk, slot):
        utail_ref[blk] = u_ref[tm:tm + 8, :]
        ca1_ref[...] = u_ref[8 + tm - a_tail:8 + tm, :]

    def qkv_begin(blk, slot):
        @pl.when(first)
        def _():
            raw_ref[slot, 0:8, :] = jnp.zeros((8, BN), F32)
            raw_ref[slot, 8 - q_tail:8, :] = cq0_ref[...]

        @pl.when(jnp.logical_not(first))
        def _():
            raw_ref[slot, 0:8, :] = qtail_ref[blk - BLK_Q]

    def qkv_piece(blk, slot, p):
        r0 = p * rp
        for hd in range(N_HEADS):
            c0 = hd * HEAD_DIM
            y = _silu(_causal_conv(raw_ref[slot, r0:r0 + 8, c0:c0 + HEAD_DIM],
                                   raw_ref[slot, 8 + r0:8 + r0 + rp, c0:c0 + HEAD_DIM],
                                   cqw_ref, (blk - BLK_Q) * D_DELTA + c0, CONV_QKV_W))
            if blk != BLK_V:
                y = y * lax.rsqrt(jnp.sum(y * y, axis=-1, keepdims=True) + EPS)
            if blk == BLK_Q:
                y = y * (HEAD_DIM ** -0.5)
            qkvz_ref[r0:r0 + rp, c0:c0 + HEAD_DIM] = y.astype(qkvz_ref.dtype)

    def qkv_end(blk, slot):
        qtail_ref[blk - BLK_Q] = raw_ref[slot, tm:tm + 8, :]
        cq1_ref[...] = raw_ref[slot, 8 + tm - q_tail:8 + tm, :]

    def zb_piece(blk, slot, p):
        r0 = p * rp
        qkvz_ref[r0:r0 + rp, :] = _silu(raw_ref[slot, 8 + r0:8 + r0 + rp, :]).astype(qkvz_ref.dtype)

    def nothing(blk, slot):
        pass

    def finisher(blk):
        if blk < BLK_Q:
            return a_begin, a_piece, a_end
        if blk < BLK_ZB:
            return qkv_begin, qkv_piece, qkv_end
        return nothing, zb_piece, nothing

    def step(jj):
        blk = jj - 1
        if blk >= 0:
            begin, piece, end = finisher(blk)
            begin(blk, blk % 2)
        for p in range(n_sub * n_piece):
            if jj < N_BLK:
                mxu_piece(jj % 2, p)
            if blk >= 0:
                piece(blk, blk % 2, p)
        if blk >= 0:
            end(blk, blk % 2)

    for jj in range(N_BLK + 1):
        pl.when(j == jj)(functools.partial(step, jj))


def _pre(h, norm_w, w_main, w_gate, conv_a_w, conv_qkv_w, ca0, cq0, *, seq_len, tm):
    n = h.shape[0]
    assert n % seq_len == 0 and seq_len % tm == 0 and tm % 64 == 0
    n_seq = n // seq_len
    tps = seq_len // tm

    def clamp(x, lo, hi):
        return jnp.minimum(jnp.maximum(x, lo), hi)

    def a_blk(j):
        return clamp(j - 1, 0, BLK_Q - 1)

    def q_blk(j):
        return clamp(j - 1 - BLK_Q, 0, 2)

    ya, qkvz, gates, ca_tiles, cq_tiles = pl.pallas_call(
        functools.partial(_pre_kernel, tm=tm, tiles_per_seq=tps),
        grid=(n // tm, N_BLK + 1),
        in_specs=[
            pl.BlockSpec((tm, D_MODEL), lambda i, j: (i, 0)),
            pl.BlockSpec((1, D_MODEL), lambda i, j: (0, 0)),
            pl.BlockSpec((D_MODEL, BN), lambda i, j: (0, jnp.minimum(j, N_BLK - 1))),
            pl.BlockSpec((D_MODEL, GATE_LANES), lambda i, j: (0, 0)),
            pl.BlockSpec((CONV_A_W, D_A), lambda i, j: (0, 0)),
            pl.BlockSpec((CONV_QKV_W, D_QKV), lambda i, j: (0, 0)),
            pl.BlockSpec((None, CONV_A_W - 1, A_Q), lambda i, j: (i // tps, 0, a_blk(j))),
            pl.BlockSpec((None, CONV_QKV_W - 1, BN), lambda i, j: (i // tps, 0, q_blk(j))),
        ],
        out_specs=[
            pl.BlockSpec((tm, A_Q), lambda i, j: (i, a_blk(j))),
            pl.BlockSpec((tm, BN), lambda i, j: (i, clamp(j - 1 - BLK_Q, 0, 3))),
            pl.BlockSpec((tm, GATE_LANES), lambda i, j: (i, 0)),
            pl.BlockSpec((None, CONV_A_W - 1, A_Q), lambda i, j: (i, 0, a_blk(j))),
            pl.BlockSpec((None, CONV_QKV_W - 1, BN), lambda i, j: (i, 0, q_blk(j))),
        ],
        out_shape=[
            jax.ShapeDtypeStruct((n, D_A), BF16),
            jax.ShapeDtypeStruct((n, D_QKVZ), BF16),
            jax.ShapeDtypeStruct((n, GATE_LANES), F32),
            jax.ShapeDtypeStruct((n // tm, CONV_A_W - 1, D_A), F32),
            jax.ShapeDtypeStruct((n // tm, CONV_QKV_W - 1, D_QKV), F32),
        ],
        scratch_shapes=[
            pltpu.VMEM((tm, D_MODEL), BF16),
            pltpu.VMEM((2, 8 + tm, BN), F32),
            pltpu.VMEM((8 + tm, A_Q), F32),
            pltpu.VMEM((3, 8, BN), F32),
            pltpu.VMEM((BLK_Q, 8, A_Q), F32),
        ],
        compiler_params=pltpu.CompilerParams(
            dimension_semantics=("arbitrary", "arbitrary"), vmem_limit_bytes=VMEM_LIMIT),
        name="pre",
    )(h, norm_w, w_main, w_gate, conv_a_w, conv_qkv_w, ca0, cq0)
    return ya, qkvz, gates, ca_tiles[tps - 1::tps], cq_tiles[tps - 1::tps]


def _outproj_kernel(ya_ref, yb_ref, h_ref, w_ref, o_ref):
    o_ref[...] = (h_ref[...] + _dot(ya_ref[...], w_ref[0:D_A, :]) + _dot(yb_ref[...], w_ref[D_A:D_MIX, :]))


def _outproj_final_kernel(ya_ref, yb_ref, h_ref, w_ref, fw_ref, o_ref):
    hn = h_ref[...] + _dot(ya_ref[...], w_ref[0:D_A, :]) + _dot(yb_ref[...], w_ref[D_A:D_MIX, :])
    ms = jnp.mean(hn * hn, axis=-1, keepdims=True)
    o_ref[...] = hn * lax.rsqrt(ms + EPS) * fw_ref[...]


def _outproj(ya, yb, h, w_out, final_w, *, tm):
    n = h.shape[0]
    assert n % tm == 0
    row = pl.BlockSpec((tm, D_MODEL), lambda i: (i, 0))
    half = pl.BlockSpec((tm, D_A), lambda i: (i, 0))
    in_specs = [half, half, row, pl.BlockSpec((D_MIX, D_MODEL), lambda i: (0, 0))]
    args = [ya, yb, h, w_out]
    body = _outproj_kernel
    if final_w is not None:
        in_specs.append(pl.BlockSpec((1, D_MODEL), lambda i: (0, 0)))
        args.append(final_w)
        body = _outproj_final_kernel
    return pl.pallas_call(
        body,
        grid=(n // tm,),
        in_specs=in_specs,
        out_specs=row,
        out_shape=jax.ShapeDtypeStruct((n, D_MODEL), F32),
        compiler_params=pltpu.CompilerParams(
            dimension_semantics=("arbitrary",), vmem_limit_bytes=VMEM_LIMIT),
        name="outproj",
    )(*args)


def _block_mask(row, col, size):
    shift = size.bit_length() - 1
    return (row >> shift) == (col >> shift)


def _inv_unit_lower(n_mats, row, col):
    eye = (row == col).astype(F32)
    in8 = _block_mask(row, col, 8)
    nd = [jnp.where(in8, n, 0.0) for n in n_mats]
    nd16 = [x.astype(BF16) for x in nd]
    m2_16 = [_dot(x, x).astype(BF16) for x in nd16]
    p = [eye - x for x in nd]
    p = [x + _dot(x.astype(BF16), m) for x, m in zip(p, m2_16)]
    m4_16 = [_dot(m, m).astype(BF16) for m in m2_16]
    t_mats = [x + _dot(x.astype(BF16), m) for x, m in zip(p, m4_16)]
    size = 8
    while size < CHUNK:
        off = jnp.logical_and(_block_mask(row, col, 2 * size), jnp.logical_not(_block_mask(row, col, size)))
        n_off16 = [jnp.where(off, n, 0.0).astype(BF16) for n in n_mats]
        t16 = [x.astype(BF16) for x in t_mats]
        a16 = [_dot(n, x).astype(BF16) for n, x in zip(n_off16, t16)]
        t_mats = [x - _dot(x16, a) for x, x16, a in zip(t_mats, t16, a16)]
        size *= 2
    return t_mats


def _gdn_kernel(qkvz_ref, gate_ref, s0_ref, gp_ref, onw_ref, y_ref, s1_ref,
                s_scr, u_scr, wq_scr, qk_scr, kdt_scr, gl_scr, *, nb, n_chunks):
    t = pl.program_id(1)

    @pl.when(t == 0)
    def _init():
        s_scr[...] = s0_ref[...]

    row = lax.broadcasted_iota(jnp.int32, (CHUNK, CHUNK), 0)
    col = lax.broadcasted_iota(jnp.int32, (CHUNK, CHUNK), 1)
    causal = row >= col
    strict = row > col
    tril = causal.astype(F32)
    probs = [(b, h) for b in range(nb) for h in range(N_HEADS)]
    n = range(len(probs))

    def head_cols(off, h):
        return slice(off + h * HEAD_DIM, off + (h + 1) * HEAD_DIM)

    def chunk_body(ci, carry):
        r0 = pl.multiple_of(ci * CHUNK, CHUNK)
        rows = pl.ds(r0, CHUNK)

        beta_t, gc, gc_t = [], [], []
        for b in range(nb):
            gl = gate_ref[b, rows, :]
            beta_t.append(_sigmoid(gl))
            g_t = -jnp.exp(gp_ref[0:1, :]) * _softplus(gl + gp_ref[1:2, :])
            gc.append(_dot_hi(tril, g_t))
            gc_t.append(gc[b].T)
            gl_scr[b, pl.ds(ci, 1), :] = gc[b][CHUNK - 1:CHUNK, :]

        q = [qkvz_ref[b, rows, head_cols(0, h)].astype(F32) for b, h in probs]
        k = [qkvz_ref[b, rows, head_cols(D_DELTA, h)].astype(F32) for b, h in probs]
        v = [qkvz_ref[b, rows, head_cols(2 * D_DELTA, h)].astype(F32) for b, h in probs]

        beta_c = [beta_t[b][:, h:h + 1] for b, h in probs]
        gc_c = [gc[b][:, N_HEADS + h:N_HEADS + h + 1] for b, h in probs]
        gc_r = [gc_t[b][N_HEADS + h:N_HEADS + h + 1, :] for b, h in probs]
        g_last = [gc[b][CHUNK - 1:CHUNK, N_HEADS + h:N_HEADS + h + 1] for b, h in probs]
        decay = [jnp.where(causal, jnp.exp(gc_c[p] - gc_r[p]), 0.0) for p in n]
        eg_c = [jnp.exp(gc_c[p]) for p in n]

        kb = [k[p] * beta_c[p] for p in n]
        k_t = [k[p].T for p in n]
        kq = [_dot(jnp.concatenate([kb[p], q[p]], axis=0).astype(BF16), k_t[p].astype(BF16))
              for p in n]
        for p, (b, h) in enumerate(probs):
            qk_scr[b, ci, h] = (kq[p][CHUNK:2 * CHUNK] * decay[p]).astype(BF16)
            kdt_scr[b, ci, h] = (k_t[p] * jnp.exp(g_last[p] - gc_r[p])).astype(BF16)
        t_mat = _inv_unit_lower([jnp.where(strict, kq[p][0:CHUNK] * decay[p], 0.0) for p in n], row, col)

        uw = [_dot(t_mat[p].astype(BF16),
                   jnp.concatenate([v[p] * beta_c[p], kb[p] * eg_c[p]], axis=1).astype(BF16))
              for p in n]
        for p, (b, h) in enumerate(probs):
            u_scr[b, ci, :, head_cols(0, h)] = uw[p][:, 0:HEAD_DIM]
            wq_scr[b, ci, h, 0:CHUNK, :] = uw[p][:, HEAD_DIM:2 * HEAD_DIM].astype(BF16)
            wq_scr[b, ci, h, CHUNK:2 * CHUNK, :] = (q[p] * eg_c[p]).astype(BF16)
        return carry

    def state_body(ci, carry):
        r0 = pl.multiple_of(ci * CHUNK, CHUNK)
        rows = pl.ds(r0, CHUNK)
        eg_last = [jnp.exp(gl_scr[b, pl.ds(ci, 1), :]) for b in range(nb)]
        s = [s_scr[b, h] for b, h in probs]
        ws = [_dot(wq_scr[b, ci, h], s[p].astype(BF16)) for p, (b, h) in enumerate(probs)]
        vn16 = [(u_scr[b, ci, :, head_cols(0, h)] - ws[p][0:CHUNK]).astype(BF16)
                for p, (b, h) in enumerate(probs)]
        for p, (b, h) in enumerate(probs):
            s_scr[b, h] = (s[p] * eg_last[b][:, N_HEADS + h:N_HEADS + h + 1]
                           + _dot(kdt_scr[b, ci, h], vn16[p]))
        o = [ws[p][CHUNK:2 * CHUNK] + _dot(qk_scr[b, ci, h], vn16[p]) for p, (b, h) in enumerate(probs)]
        for p, (b, h) in enumerate(probs):
            ms = jnp.mean(o[p] * o[p], axis=-1, keepdims=True)
            on = o[p] * lax.rsqrt(ms + EPS) * onw_ref[...]
            gate = qkvz_ref[b, rows, head_cols(D_QKV, h)].astype(F32)
            y_ref[b, rows, head_cols(0, h)] = (on * gate).astype(y_ref.dtype)
        return carry

    lax.fori_loop(0, n_chunks, chunk_body, 0)
    lax.fori_loop(0, n_chunks, state_body, 0)

    @pl.when(t == pl.num_programs(1) - 1)
    def _fin():
        s1_ref[...] = s_scr[...]


def _gdn(qkvz, gates, s0, gate_par, o_norm_w, *, nb, tile):
    b, l, _ = qkvz.shape
    assert b % nb == 0 and l % tile == 0 and tile % CHUNK == 0
    n_chunks = tile // CHUNK

    def seq(width):
        return pl.BlockSpec((nb, tile, width), lambda i, t: (i, t, 0))

    st_s = (N_HEADS, HEAD_DIM, HEAD_DIM)
    state = pl.BlockSpec((nb,) + st_s, lambda i, t: (i, 0, 0, 0))
    return pl.pallas_call(
        functools.partial(_gdn_kernel, nb=nb, n_chunks=n_chunks),
        grid=(b // nb, l // tile),
        in_specs=[
            seq(D_QKVZ), seq(GATE_LANES), state,
            pl.BlockSpec((2, GATE_LANES), lambda i, t: (0, 0)),
            pl.BlockSpec((1, HEAD_DIM), lambda i, t: (0, 0)),
        ],
        out_specs=[seq(D_DELTA), state],
        out_shape=[
            jax.ShapeDtypeStruct((b, l, D_DELTA), BF16),
            jax.ShapeDtypeStruct((b,) + st_s, F32),
        ],
        scratch_shapes=[
            pltpu.VMEM((nb,) + st_s, F32),
            pltpu.VMEM((nb, n_chunks, CHUNK, D_DELTA), F32),
            pltpu.VMEM((nb, n_chunks, N_HEADS, 2 * CHUNK, HEAD_DIM), BF16),
            pltpu.VMEM((nb, n_chunks, N_HEADS, CHUNK, CHUNK), BF16),
            pltpu.VMEM((nb, n_chunks, N_HEADS, HEAD_DIM, CHUNK), BF16),
            pltpu.VMEM((nb, max(n_chunks, 8), GATE_LANES), F32),
        ],
        compiler_params=pltpu.CompilerParams(
            dimension_semantics=("arbitrary", "arbitrary"), vmem_limit_bytes=VMEM_LIMIT),
        name="gdn",
    )(qkvz, gates, s0, gate_par, o_norm_w)


def _pick_tile(n, target):
    t = min(n, target)
    while n % t:
        t //= 2
    return t


def kernel(x_prompt, x_sample, state_conv_a, state_conv_qkv, state_delta, meta_tokens, norm_w, w_in,
           conv_a_w, conv_qkv_w, a_log, dt_bias, o_norm_w, w_out, final_norm_w):
    b, seq, d = x_prompt.shape
    bs, dec_seq, _ = x_sample.shape
    depth = norm_w.shape[0]
    assert d == D_MODEL and dec_seq == CHUNK and seq % CHUNK == 0

    meta_blk = jnp.concatenate([jnp.zeros((CHUNK - N_META, d), F32), meta_tokens.astype(F32)], axis=0)
    h_small = jnp.concatenate([jnp.broadcast_to(meta_blk[None], (b, CHUNK, d)), x_sample], axis=0)
    h_small = h_small.reshape((b + bs) * CHUNK, d)
    h_main = x_prompt.reshape(b * seq, d)
    n_small, n_main = h_small.shape[0], h_main.shape[0]

    p_a, p_qkv, p_s, s_a, s_qkv, s_s = [], [], [], [], [], []
    for layer in range(depth):
        w_a = w_in[layer, :, :4 * D_A].reshape(d, 4, BLK_Q, A_Q).transpose(0, 2, 1, 3).reshape(d, 4 * D_A)
        w_main = jnp.concatenate([w_a, w_in[layer, :, 4 * D_A:D_MAIN]], axis=1).astype(BF16)
        w_gate = jnp.pad(w_in[layer, :, D_MAIN:], ((0, 0), (0, GATE_LANES - 2 * N_HEADS))).astype(BF16)
        w_o = w_out[layer].astype(BF16)
        nw = norm_w[layer].reshape(1, d)
        gate_par = jnp.zeros((2, GATE_LANES), F32)
        gate_par = gate_par.at[0, N_HEADS:2 * N_HEADS].set(a_log[layer])
        gate_par = gate_par.at[1, N_HEADS:2 * N_HEADS].set(dt_bias[layer])
        onw = o_norm_w[layer].reshape(1, HEAD_DIM)
        fw = final_norm_w.reshape(1, d) if layer == depth - 1 else None
        pre = functools.partial(_pre, norm_w=nw, w_main=w_main, w_gate=w_gate,
                                conv_a_w=conv_a_w[layer], conv_qkv_w=conv_qkv_w[layer])
        gdn = functools.partial(_gdn, gate_par=gate_par, o_norm_w=onw)

        ca0 = jnp.concatenate([jnp.zeros((b,) + state_conv_a.shape[2:], F32), state_conv_a[layer]], axis=0)
        cq0 = jnp.concatenate([jnp.zeros((b,) + state_conv_qkv.shape[2:], F32), state_conv_qkv[layer]], axis=0)
        s0 = jnp.concatenate([jnp.zeros((b,) + state_delta.shape[2:], F32), state_delta[layer]], axis=0)
        ya, qkvz, gates, ca1, cq1 = pre(h_small, ca0=ca0, cq0=cq0, seq_len=CHUNK, tm=CHUNK)
        yb, s1 = gdn(qkvz.reshape(b + bs, CHUNK, D_QKVZ), gates.reshape(b + bs, CHUNK, GATE_LANES), s0,
                     nb=_pick_tile(b + bs, GDN_BATCH), tile=CHUNK)
        h_small = _outproj(ya, yb.reshape(n_small, D_DELTA), h_small, w_o, fw, tm=_pick_tile(n_small, 512))
        s_a.append(ca1[b:]); s_qkv.append(cq1[b:]); s_s.append(s1[b:])

        ya, qkvz, gates, ca2, cq2 = pre(h_main, ca0=ca1[:b], cq0=cq1[:b], seq_len=seq,
                                        tm=_pick_tile(seq, 1024))
        yb, s2 = gdn(qkvz.reshape(b, seq, D_QKVZ), gates.reshape(b, seq, GATE_LANES), s1[:b],
                     nb=_pick_tile(b, GDN_BATCH), tile=_pick_tile(seq, GDN_TILE))
        h_main = _outproj(ya, yb.reshape(n_main, D_DELTA), h_main, w_o, fw, tm=_pick_tile(n_main, 512))
        p_a.append(ca2); p_qkv.append(cq2); p_s.append(s2)

    y_prompt = h_main.reshape(b, seq, d)
    y_sample = h_small.reshape(b + bs, CHUNK, d)[b:]
    return (y_prompt, y_sample, jnp.stack(p_a), jnp.stack(p_qkv), jnp.stack(p_s),
            jnp.stack(s_a), jnp.stack(s_qkv), jnp.stack(s_s))
```

```python
import functools

import jax
import jax.numpy as jnp
from jax import lax
from jax.experimental import pallas as pl
from jax.experimental.pallas import tpu as pltpu

D_MODEL = 2048
N_META = 16
CHUNK = 64
D_A = 1024
N_HEADS = 8
HEAD_DIM = 128
D_DELTA = N_HEADS * HEAD_DIM
D_QKV = 3 * D_DELTA
D_MAIN = 4 * D_A + D_QKV + D_DELTA
D_MIX = D_A + D_DELTA
GATE_LANES = 128
CONV_A_W = 3
CONV_QKV_W = 4
EPS = 1e-6

F32 = jnp.float32
BF16 = jnp.bfloat16

BN = 1024
PRE_BLOCKS = 4
A_Q = D_A // PRE_BLOCKS
D_QKVZ = D_QKV + D_DELTA
PIECE = 256
SUB_ROWS = 256
GDN_BATCH = 4
GDN_TILE = 256

VMEM_LIMIT = 56 * 1024 * 1024


def _sigmoid(x):
    return 0.5 + 0.5 * jnp.tanh(0.5 * x)


def _silu(x):
    hx = 0.5 * x
    return hx + hx * jnp.tanh(hx)


def _softplus(x):
    return jnp.maximum(x, 0.0) + jnp.log1p(jnp.exp(-jnp.abs(x)))


def _dot(a, b):
    return jnp.dot(a, b, preferred_element_type=F32)


def _dot_hi(a, b):
    return jnp.dot(a, b, preferred_element_type=F32, precision=lax.Precision.HIGHEST)


def _causal_conv(prev8, x, w_ref, width):
    xs = jnp.concatenate([prev8, x], axis=0)
    acc = w_ref[width - 1:width, :] * xs
    for s in range(1, width):
        acc = acc + w_ref[width - 1 - s:width - s, :] * pltpu.roll(xs, s, 0)
    return acc[8:]


def _norm_tile(h_ref, nw_ref, xn_ref):
    x = h_ref[...]
    ms = jnp.mean(x * x, axis=-1, keepdims=True)
    xn = (x * lax.rsqrt(ms + EPS) * nw_ref[...]).astype(BF16)
    xn_ref[...] = xn
    return xn


def _project_and_finish(xn_ref, w_ref, raw_ref, acc_ref, finish, *, tm):
    sub = min(SUB_ROWS, tm)
    n_sub = tm // sub
    n_piece = BN // PIECE
    rp = sub // n_piece

    def body(r, carry):
        rbase = pl.multiple_of(r * sub, sub)
        ebase = jnp.maximum(r - 1, 0) * sub
        acc_ref[...] = _dot(xn_ref[pl.ds(rbase, sub), :], w_ref[...])
        for c in range(n_piece):
            finish(pl.multiple_of(ebase + c * rp, rp), rp)
        raw_ref[pl.ds(pl.multiple_of(8 + rbase, 8), sub), :] = acc_ref[...]
        return carry

    lax.fori_loop(0, n_sub, body, 0)
    for c in range(n_piece):
        finish((n_sub - 1) * sub + c * rp, rp)


def _pre_a_kernel(h_ref, nw_ref, w_ref, wg_ref, caw_ref, ca0_ref, ya_ref, gate_ref, ca1_ref,
                  xn_ref, raw_ref, acc_ref, u_ref, utail_ref, *, tm, tiles_per_seq):
    i = pl.program_id(0)
    j = pl.program_id(1)
    first = (i % tiles_per_seq) == 0
    a_tail = CONV_A_W - 1

    @pl.when(j == 0)
    def _norm():
        gate_ref[...] = _dot(_norm_tile(h_ref, nw_ref, xn_ref), wg_ref[...])

    @pl.when(jnp.logical_and(i == 0, j == 0))
    def _clear():
        raw_ref[...] = jnp.zeros(raw_ref.shape, F32)
        u_ref[...] = jnp.zeros(u_ref.shape, F32)

    @pl.when(first)
    def _from_state():
        u_ref[0:8, :] = jnp.zeros((8, A_Q), F32)
        u_ref[8 - a_tail:8, :] = ca0_ref[...]

    @pl.when(jnp.logical_not(first))
    def _from_prev_tile():
        u_ref[0:8, :] = utail_ref[j]

    def finish(e0, rows):
        rb = raw_ref[pl.ds(8 + e0, rows), :]
        u = rb[:, A_Q:2 * A_Q] * rb[:, 2 * A_Q:3 * A_Q]
        u_ref[pl.ds(8 + e0, rows), :] = u
        conv = _causal_conv(u_ref[pl.ds(e0, 8), :], u, caw_ref, CONV_A_W)
        ya_ref[pl.ds(e0, rows), :] = (rb[:, 0:A_Q] * conv * _silu(rb[:, 3 * A_Q:4 * A_Q])).astype(ya_ref.dtype)

    _project_and_finish(xn_ref, w_ref, raw_ref, acc_ref, finish, tm=tm)
    utail_ref[j] = u_ref[tm:tm + 8, :]
    ca1_ref[...] = u_ref[8 + tm - a_tail:8 + tm, :]


def _pre_q_kernel(h_ref, nw_ref, w_ref, cqw_ref, par_ref, cq0_ref, qkvz_ref, cq1_ref,
                  xn_ref, raw_ref, acc_ref, qtail_ref, *, tm, tiles_per_seq):
    i = pl.program_id(0)
    j = pl.program_id(1)
    first = (i % tiles_per_seq) == 0
    q_tail = CONV_QKV_W - 1

    @pl.when(j == 0)
    def _norm():
        _norm_tile(h_ref, nw_ref, xn_ref)

    @pl.when(jnp.logical_and(i == 0, j == 0))
    def _clear():
        raw_ref[...] = jnp.zeros(raw_ref.shape, F32)

    @pl.when(first)
    def _from_state():
        raw_ref[0:8, :] = jnp.zeros((8, BN), F32)
        raw_ref[8 - q_tail:8, :] = cq0_ref[...]

    @pl.when(jnp.logical_not(first))
    def _from_prev_tile():
        raw_ref[0:8, :] = qtail_ref[j]

    use_norm = par_ref[0:1, 0:1]
    scale = par_ref[1:2, 0:1]

    def finish(e0, rows):
        for hd in range(N_HEADS):
            cols = slice(hd * HEAD_DIM, (hd + 1) * HEAD_DIM)
            y = _silu(_causal_conv(raw_ref[pl.ds(e0, 8), cols], raw_ref[pl.ds(8 + e0, rows), cols],
                                   cqw_ref.at[:, cols], CONV_QKV_W))
            r = lax.rsqrt(jnp.sum(y * y, axis=-1, keepdims=True) + EPS)
            qkvz_ref[pl.ds(e0, rows), cols] = (y * ((1.0 + use_norm * (r - 1.0)) * scale)).astype(qkvz_ref.dtype)

    _project_and_finish(xn_ref, w_ref, raw_ref, acc_ref, finish, tm=tm)
    qtail_ref[j] = raw_ref[tm:tm + 8, :]

    @pl.when(j < 3)
    def _state():
        cq1_ref[...] = raw_ref[8 + tm - q_tail:8 + tm, :]


def _pre(h, norm_w, w_a, w_q, w_gate, conv_a_w, conv_q_w, q_par, ca0, cq0, *, seq_len, tm):
    n = h.shape[0]
    assert n % seq_len == 0 and seq_len % tm == 0 and tm % CHUNK == 0
    tps = seq_len // tm
    n_tiles = n // tm
    assert w_a.shape[1] == PRE_BLOCKS * BN and w_q.shape[1] == PRE_BLOCKS * BN
    params = pltpu.CompilerParams(dimension_semantics=("arbitrary", "arbitrary"), vmem_limit_bytes=VMEM_LIMIT)
    h_spec = pl.BlockSpec((tm, D_MODEL), lambda i, j: (i, 0))
    nw_spec = pl.BlockSpec((1, D_MODEL), lambda i, j: (0, 0))
    w_spec = pl.BlockSpec((D_MODEL, BN), lambda i, j: (0, j))

    ya, gates, ca_tiles = pl.pallas_call(
        functools.partial(_pre_a_kernel, tm=tm, tiles_per_seq=tps),
        grid=(n_tiles, PRE_BLOCKS),
        in_specs=[
            h_spec, nw_spec, w_spec,
            pl.BlockSpec((D_MODEL, GATE_LANES), lambda i, j: (0, 0)),
            pl.BlockSpec((CONV_A_W, A_Q), lambda i, j: (0, j)),
            pl.BlockSpec((None, CONV_A_W - 1, A_Q), lambda i, j: (i // tps, 0, j)),
        ],
        out_specs=[
            pl.BlockSpec((tm, A_Q), lambda i, j: (i, j)),
            pl.BlockSpec((tm, GATE_LANES), lambda i, j: (i, 0)),
            pl.BlockSpec((None, CONV_A_W - 1, A_Q), lambda i, j: (i, 0, j)),
        ],
        out_shape=[
            jax.ShapeDtypeStruct((n, D_A), BF16),
            jax.ShapeDtypeStruct((n, GATE_LANES), F32),
            jax.ShapeDtypeStruct((n_tiles, CONV_A_W - 1, D_A), F32),
        ],
        scratch_shapes=[
            pltpu.VMEM((tm, D_MODEL), BF16),
            pltpu.VMEM((8 + tm, BN), F32),
            pltpu.VMEM((min(SUB_ROWS, tm), BN), F32),
            pltpu.VMEM((8 + tm, A_Q), F32),
            pltpu.VMEM((PRE_BLOCKS, 8, A_Q), F32),
        ],
        compiler_params=params,
        name="pre_a",
    )(h, norm_w, w_a, w_gate, conv_a_w, ca0)

    qkvz, cq_tiles = pl.pallas_call(
        functools.partial(_pre_q_kernel, tm=tm, tiles_per_seq=tps),
        grid=(n_tiles, PRE_BLOCKS),
        in_specs=[
            h_spec, nw_spec, w_spec,
            pl.BlockSpec((CONV_QKV_W, BN), lambda i, j: (0, j)),
            pl.BlockSpec((2, BN), lambda i, j: (0, j)),
            pl.BlockSpec((None, CONV_QKV_W - 1, BN), lambda i, j: (i // tps, 0, jnp.minimum(j, 2))),
        ],
        out_specs=[
            pl.BlockSpec((tm, BN), lambda i, j: (i, j)),
            pl.BlockSpec((None, CONV_QKV_W - 1, BN), lambda i, j: (i, 0, jnp.minimum(j, 2))),
        ],
        out_shape=[
            jax.ShapeDtypeStruct((n, D_QKVZ), BF16),
            jax.ShapeDtypeStruct((n_tiles, CONV_QKV_W - 1, D_QKV), F32),
        ],
        scratch_shapes=[
            pltpu.VMEM((tm, D_MODEL), BF16),
            pltpu.VMEM((8 + tm, BN), F32),
            pltpu.VMEM((min(SUB_ROWS, tm), BN), F32),
            pltpu.VMEM((PRE_BLOCKS, 8, BN), F32),
        ],
        compiler_params=params,
        name="pre_q",
    )(h, norm_w, w_q, conv_q_w, q_par, cq0)
    return ya, qkvz, gates, ca_tiles[tps - 1::tps], cq_tiles[tps - 1::tps]


def _outproj_kernel(ya_ref, yb_ref, h_ref, w_ref, o_ref):
    o_ref[...] = (h_ref[...] + _dot(ya_ref[...], w_ref[0:D_A, :]) + _dot(yb_ref[...], w_ref[D_A:D_MIX, :]))


def _outproj_final_kernel(ya_ref, yb_ref, h_ref, w_ref, fw_ref, o_ref):
    hn = h_ref[...] + _dot(ya_ref[...], w_ref[0:D_A, :]) + _dot(yb_ref[...], w_ref[D_A:D_MIX, :])
    ms = jnp.mean(hn * hn, axis=-1, keepdims=True)
    o_ref[...] = hn * lax.rsqrt(ms + EPS) * fw_ref[...]


def _outproj(ya, yb, h, w_out, final_w, *, tm):
    n = h.shape[0]
    assert n % tm == 0
    row = pl.BlockSpec((tm, D_MODEL), lambda i: (i, 0))
    half = pl.BlockSpec((tm, D_A), lambda i: (i, 0))
    in_specs = [half, half, row, pl.BlockSpec((D_MIX, D_MODEL), lambda i: (0, 0))]
    args = [ya, yb, h, w_out]
    body = _outproj_kernel
    if final_w is not None:
        in_specs.append(pl.BlockSpec((1, D_MODEL), lambda i: (0, 0)))
        args.append(final_w)
        body = _outproj_final_kernel
    return pl.pallas_call(
        body,
        grid=(n // tm,),
        in_specs=in_specs,
        out_specs=row,
        out_shape=jax.ShapeDtypeStruct((n, D_MODEL), F32),
        compiler_params=pltpu.CompilerParams(
            dimension_semantics=("arbitrary",), vmem_limit_bytes=VMEM_LIMIT),
        name="outproj",
    )(*args)


def _block_mask(row, col, size):
    shift = size.bit_length() - 1
    return (row >> shift) == (col >> shift)


def _inv_unit_lower(n_mats, row, col):
    eye = (row == col).astype(F32)
    in8 = _block_mask(row, col, 8)
    nd = [jnp.where(in8, n, 0.0) for n in n_mats]
    nd16 = [x.astype(BF16) for x in nd]
    m2_16 = [_dot(x, x).astype(BF16) for x in nd16]
    p = [eye - x for x in nd]
    p = [x + _dot(x.astype(BF16), m) for x, m in zip(p, m2_16)]
    m4_16 = [_dot(m, m).astype(BF16) for m in m2_16]
    t_mats = [x + _dot(x.astype(BF16), m) for x, m in zip(p, m4_16)]
    size = 8
    while size < CHUNK:
        off = jnp.logical_and(_block_mask(row, col, 2 * size), jnp.logical_not(_block_mask(row, col, size)))
        n_off16 = [jnp.where(off, n, 0.0).astype(BF16) for n in n_mats]
        t16 = [x.astype(BF16) for x in t_mats]
        a16 = [_dot(n, x).astype(BF16) for n, x in zip(n_off16, t16)]
        t_mats = [x - _dot(x16, a) for x, x16, a in zip(t_mats, t16, a16)]
        size *= 2
    return t_mats


def _gdn_kernel(qkvz_ref, gate_ref, s0_ref, gp_ref, onw_ref, y_ref, s1_ref,
                s_scr, u_scr, wq_scr, qk_scr, kdt_scr, gl_scr, *, nb, n_chunks):
    t = pl.program_id(1)

    @pl.when(t == 0)
    def _init():
        s_scr[...] = s0_ref[...]

    row = lax.broadcasted_iota(jnp.int32, (CHUNK, CHUNK), 0)
    col = lax.broadcasted_iota(jnp.int32, (CHUNK, CHUNK), 1)
    causal = row >= col
    strict = row > col
    tril = causal.astype(F32)
    probs = [(b, h) for b in range(nb) for h in range(N_HEADS)]
    n = range(len(probs))

    def head_cols(off, h):
        return slice(off + h * HEAD_DIM, off + (h + 1) * HEAD_DIM)

    def chunk_body(ci, carry):
        r0 = pl.multiple_of(ci * CHUNK, CHUNK)
        rows = pl.ds(r0, CHUNK)

        beta_t, gc, gc_t = [], [], []
        for b in range(nb):
            gl = gate_ref[b, rows, :]
            beta_t.append(_sigmoid(gl))
            g_t = -jnp.exp(gp_ref[0:1, :]) * _softplus(gl + gp_ref[1:2, :])
            gc.append(_dot_hi(tril, g_t))
            gc_t.append(gc[b].T)
            gl_scr[b, pl.ds(ci, 1), :] = gc[b][CHUNK - 1:CHUNK, :]

        q = [qkvz_ref[b, rows, head_cols(0, h)].astype(F32) for b, h in probs]
        k = [qkvz_ref[b, rows, head_cols(D_DELTA, h)].astype(F32) for b, h in probs]
        v = [qkvz_ref[b, rows, head_cols(2 * D_DELTA, h)].astype(F32) for b, h in probs]

        beta_c = [beta_t[b][:, h:h + 1] for b, h in probs]
        gc_c = [gc[b][:, N_HEADS + h:N_HEADS + h + 1] for b, h in probs]
        gc_r = [gc_t[b][N_HEADS + h:N_HEADS + h + 1, :] for b, h in probs]
        g_last = [gc[b][CHUNK - 1:CHUNK, N_HEADS + h:N_HEADS + h + 1] for b, h in probs]
        decay = [jnp.where(causal, jnp.exp(gc_c[p] - gc_r[p]), 0.0) for p in n]
        eg_c = [jnp.exp(gc_c[p]) for p in n]

        kb = [k[p] * beta_c[p] for p in n]
        k_t = [k[p].T for p in n]
        kq = [_dot(jnp.concatenate([kb[p], q[p]], axis=0).astype(BF16), k_t[p].astype(BF16))
              for p in n]
        for p, (b, h) in enumerate(probs):
            qk_scr[b, ci, h] = (kq[p][CHUNK:2 * CHUNK] * decay[p]).astype(BF16)
            kdt_scr[b, ci, h] = (k_t[p] * jnp.exp(g_last[p] - gc_r[p])).astype(BF16)
        t_mat = _inv_unit_lower([jnp.where(strict, kq[p][0:CHUNK] * decay[p], 0.0) for p in n], row, col)

        uw = [_dot(t_mat[p].astype(BF16),
                   jnp.concatenate([v[p] * beta_c[p], kb[p] * eg_c[p]], axis=1).astype(BF16))
              for p in n]
        for p, (b, h) in enumerate(probs):
            u_scr[b, ci, :, head_cols(0, h)] = uw[p][:, 0:HEAD_DIM]
            wq_scr[b, ci, h, 0:CHUNK, :] = uw[p][:, HEAD_DIM:2 * HEAD_DIM].astype(BF16)
            wq_scr[b, ci, h, CHUNK:2 * CHUNK, :] = (q[p] * eg_c[p]).astype(BF16)
        return carry

    def state_body(ci, carry):
        r0 = pl.multiple_of(ci * CHUNK, CHUNK)
        rows = pl.ds(r0, CHUNK)
        eg_last = [jnp.exp(gl_scr[b, pl.ds(ci, 1), :]) for b in range(nb)]
        s = [s_scr[b, h] for b, h in probs]
        ws = [_dot(wq_scr[b, ci, h], s[p].astype(BF16)) for p, (b, h) in enumerate(probs)]
        vn16 = [(u_scr[b, ci, :, head_cols(0, h)] - ws[p][0:CHUNK]).astype(BF16)
                for p, (b, h) in enumerate(probs)]
        for p, (b, h) in enumerate(probs):
            s_scr[b, h] = (s[p] * eg_last[b][:, N_HEADS + h:N_HEADS + h + 1]
                           + _dot(kdt_scr[b, ci, h], vn16[p]))
        o = [ws[p][CHUNK:2 * CHUNK] + _dot(qk_scr[b, ci, h], vn16[p]) for p, (b, h) in enumerate(probs)]
        for p, (b, h) in enumerate(probs):
            ms = jnp.mean(o[p] * o[p], axis=-1, keepdims=True)
            on = o[p] * lax.rsqrt(ms + EPS) * onw_ref[...]
            gate = qkvz_ref[b, rows, head_cols(D_QKV, h)].astype(F32)
            y_ref[b, rows, head_cols(0, h)] = (on * gate).astype(y_ref.dtype)
        return carry

    lax.fori_loop(0, n_chunks, chunk_body, 0)
    lax.fori_loop(0, n_chunks, state_body, 0)

    @pl.when(t == pl.num_programs(1) - 1)
    def _fin():
        s1_ref[...] = s_scr[...]


def _gdn(qkvz, gates, s0, gate_par, o_norm_w, *, nb, tile):
    b, l, _ = qkvz.shape
    assert b % nb == 0 and l % tile == 0 and tile % CHUNK == 0
    n_chunks = tile // CHUNK

    def seq(width):
        return pl.BlockSpec((nb, tile, width), lambda i, t: (i, t, 0))

    st_s = (N_HEADS, HEAD_DIM, HEAD_DIM)
    state = pl.BlockSpec((nb,) + st_s, lambda i, t: (i, 0, 0, 0))
    return pl.pallas_call(
        functools.partial(_gdn_kernel, nb=nb, n_chunks=n_chunks),
        grid=(b // nb, l // tile),
        in_specs=[
            seq(D_QKVZ), seq(GATE_LANES), state,
            pl.BlockSpec((2, GATE_LANES), lambda i, t: (0, 0)),
            pl.BlockSpec((1, HEAD_DIM), lambda i, t: (0, 0)),
        ],
        out_specs=[seq(D_DELTA), state],
        out_shape=[
            jax.ShapeDtypeStruct((b, l, D_DELTA), BF16),
            jax.ShapeDtypeStruct((b,) + st_s, F32),
        ],
        scratch_shapes=[
            pltpu.VMEM((nb,) + st_s, F32),
            pltpu.VMEM((nb, n_chunks, CHUNK, D_DELTA), F32),
            pltpu.VMEM((nb, n_chunks, N_HEADS, 2 * CHUNK, HEAD_DIM), BF16),
            pltpu.VMEM((nb, n_chunks, N_HEADS, CHUNK, CHUNK), BF16),
            pltpu.VMEM((nb, n_chunks, N_HEADS, HEAD_DIM, CHUNK), BF16),
            pltpu.VMEM((nb, max(n_chunks, 8), GATE_LANES), F32),
        ],
        compiler_params=pltpu.CompilerParams(
            dimension_semantics=("arbitrary", "arbitrary"), vmem_limit_bytes=VMEM_LIMIT),
        name="gdn",
    )(qkvz, gates, s0, gate_par, o_norm_w)


def _pick_tile(n, target):
    t = min(n, target)
    while n % t:
        t //= 2
    return t


def kernel(x_prompt, x_sample, state_conv_a, state_conv_qkv, state_delta, meta_tokens, norm_w, w_in,
           conv_a_w, conv_qkv_w, a_log, dt_bias, o_norm_w, w_out, final_norm_w):
    b, seq, d = x_prompt.shape
    bs, dec_seq, _ = x_sample.shape
    depth = norm_w.shape[0]
    assert d == D_MODEL and dec_seq == CHUNK and seq % CHUNK == 0

    meta_blk = jnp.concatenate([jnp.zeros((CHUNK - N_META, d), F32), meta_tokens.astype(F32)], axis=0)
    h_small = jnp.concatenate([jnp.broadcast_to(meta_blk[None], (b, CHUNK, d)), x_sample], axis=0)
    h_small = h_small.reshape((b + bs) * CHUNK, d)
    h_main = x_prompt.reshape(b * seq, d)
    n_small, n_main = h_small.shape[0], h_main.shape[0]

    zb_taps = jnp.zeros((CONV_QKV_W, D_DELTA), F32).at[CONV_QKV_W - 1].set(1.0)
    q_par = jnp.stack([
        jnp.concatenate([jnp.ones((2 * D_DELTA,), F32), jnp.zeros((2 * D_DELTA,), F32)]),
        jnp.concatenate([jnp.full((D_DELTA,), HEAD_DIM ** -0.5, F32), jnp.ones((3 * D_DELTA,), F32)]),
    ])

    p_a, p_qkv, p_s, s_a, s_qkv, s_s = [], [], [], [], [], []
    for layer in range(depth):
        w_a = (w_in[layer, :, :4 * D_A].reshape(d, 4, PRE_BLOCKS, A_Q).transpose(0, 2, 1, 3)
               .reshape(d, 4 * D_A).astype(BF16))
        w_q = w_in[layer, :, 4 * D_A:D_MAIN].astype(BF16)
        w_gate = jnp.pad(w_in[layer, :, D_MAIN:], ((0, 0), (0, GATE_LANES - 2 * N_HEADS))).astype(BF16)
        w_o = w_out[layer].astype(BF16)
        nw = norm_w[layer].reshape(1, d)
        gate_par = jnp.zeros((2, GATE_LANES), F32)
        gate_par = gate_par.at[0, N_HEADS:2 * N_HEADS].set(a_log[layer])
        gate_par = gate_par.at[1, N_HEADS:2 * N_HEADS].set(dt_bias[layer])
        onw = o_norm_w[layer].reshape(1, HEAD_DIM)
        fw = final_norm_w.reshape(1, d) if layer == depth - 1 else None
        pre = functools.partial(_pre, norm_w=nw, w_a=w_a, w_q=w_q, w_gate=w_gate, conv_a_w=conv_a_w[layer],
                                conv_q_w=jnp.concatenate([conv_qkv_w[layer], zb_taps], axis=1), q_par=q_par)
        gdn = functools.partial(_gdn, gate_par=gate_par, o_norm_w=onw)

        ca0 = jnp.concatenate([jnp.zeros((b,) + state_conv_a.shape[2:], F32), state_conv_a[layer]], axis=0)
        cq0 = jnp.concatenate([jnp.zeros((b,) + state_conv_qkv.shape[2:], F32), state_conv_qkv[layer]], axis=0)
        s0 = jnp.concatenate([jnp.zeros((b,) + state_delta.shape[2:], F32), state_delta[layer]], axis=0)
        ya, qkvz, gates, ca1, cq1 = pre(h_small, ca0=ca0, cq0=cq0, seq_len=CHUNK, tm=CHUNK)
        yb, s1 = gdn(qkvz.reshape(b + bs, CHUNK, D_QKVZ), gates.reshape(b + bs, CHUNK, GATE_LANES), s0,
                     nb=_pick_tile(b + bs, GDN_BATCH), tile=CHUNK)
        h_small = _outproj(ya, yb.reshape(n_small, D_DELTA), h_small, w_o, fw, tm=_pick_tile(n_small, 512))
        s_a.append(ca1[b:]); s_qkv.append(cq1[b:]); s_s.append(s1[b:])

        ya, qkvz, gates, ca2, cq2 = pre(h_main, ca0=ca1[:b], cq0=cq1[:b], seq_len=seq,
                                        tm=_pick_tile(seq, 1024))
        yb, s2 = gdn(qkvz.reshape(b, seq, D_QKVZ), gates.reshape(b, seq, GATE_LANES), s1[:b],
                     nb=_pick_tile(b, GDN_BATCH), tile=_pick_tile(seq, GDN_TILE))
        h_main = _outproj(ya, yb.reshape(n_main, D_DELTA), h_main, w_o, fw, tm=_pick_tile(n_main, 512))
        p_a.append(ca2); p_qkv.append(cq2); p_s.append(s2)

    y_prompt = h_main.reshape(b, seq, d)
    y_sample = h_small.reshape(b + bs, CHUNK, d)[b:]
    return (y_prompt, y_sample, jnp.stack(p_a), jnp.stack(p_qkv), jnp.stack(p_s),
            jnp.stack(s_a), jnp.stack(s_qkv), jnp.stack(s_s))
```

```python
import functools

import jax
import jax.numpy as jnp
from jax import lax
from jax.experimental import pallas as pl
from jax.experimental.pallas import tpu as pltpu

D_MODEL = 2048
N_META = 16
CHUNK = 64
D_A = 1024
N_HEADS = 8
HEAD_DIM = 128
D_DELTA = N_HEADS * HEAD_DIM
D_QKV = 3 * D_DELTA
D_MAIN = 4 * D_A + D_QKV + D_DELTA
D_MIX = D_A + D_DELTA
GATE_LANES = 128
CONV_A_W = 3
CONV_QKV_W = 4
EPS = 1e-6

F32 = jnp.float32
BF16 = jnp.bfloat16

BN = 1024
PRE_BLOCKS = 4
A_Q = D_A // PRE_BLOCKS
D_QKVZ = D_QKV + D_DELTA
SUB_ROWS = 512
EP_ROWS = 64
NORM_ROWS = 128
GDN_BATCH = 4
GDN_TILE = 256

VMEM_LIMIT = 56 * 1024 * 1024


def _sigmoid(x):
    return 0.5 + 0.5 * jnp.tanh(0.5 * x)


def _silu(x):
    hx = 0.5 * x
    return hx + hx * jnp.tanh(hx)


def _softplus(x):
    return jnp.maximum(x, 0.0) + jnp.log1p(jnp.exp(-jnp.abs(x)))


def _dot(a, b):
    return jnp.dot(a, b, preferred_element_type=F32)


def _dot_hi(a, b):
    return jnp.dot(a, b, preferred_element_type=F32, precision=lax.Precision.HIGHEST)


def _causal_conv(prev8, x, w_ref, width):
    xs = jnp.concatenate([prev8, x], axis=0)
    acc = w_ref[width - 1:width, :] * xs
    for s in range(1, width):
        acc = acc + w_ref[width - 1 - s:width - s, :] * pltpu.roll(xs, s, 0)
    return acc[8:]


def _norm_tile(h_ref, nw_ref, xn_ref, *, tm):
    rows = min(NORM_ROWS, tm)

    def body(r, carry):
        sl = pl.ds(pl.multiple_of(r * rows, rows), rows)
        x = h_ref[sl, :]
        ms = jnp.mean(x * x, axis=-1, keepdims=True)
        xn_ref[sl, :] = (x * lax.rsqrt(ms + EPS) * nw_ref[...]).astype(xn_ref.dtype)
        return carry

    lax.fori_loop(0, tm // rows, body, 0)


def _lagged(s):
    sp = jnp.maximum(s - 1, 0)
    return sp // PRE_BLOCKS, sp % PRE_BLOCKS


def _project_and_finish(project, finish, raw_ref, acc_ref, *, tm):
    cur = pl.program_id(0) % 2
    sub = min(SUB_ROWS, tm)
    rp = min(EP_ROWS, sub)

    def body(r, carry):
        rbase = pl.multiple_of(r * sub, sub)
        project(rbase, sub)
        for p in range(sub // rp):
            finish(pl.multiple_of(rbase + p * rp, rp), rp)
        raw_ref[cur, pl.ds(pl.multiple_of(8 + rbase, 8), sub), :] = acc_ref[...]
        return carry

    lax.fori_loop(0, tm // sub, body, 0)


def _pre_a_kernel(h_ref, nw_ref, wb_ref, wc_ref, wx_ref, wz_ref, wg_ref, caw_ref, ca0_ref,
                  ya_ref, xn_ref, gate_ref, ca1_ref,
                  raw_ref, acc_ref, u_ref, utail_ref, *, tm, tiles_per_seq):
    s = pl.program_id(0)
    ip, jp = _lagged(s)
    prev = 1 - s % 2
    first = (ip % tiles_per_seq) == 0
    a_tail = CONV_A_W - 1

    @pl.when(s == 0)
    def _clear():
        raw_ref[...] = jnp.zeros(raw_ref.shape, F32)
        u_ref[...] = jnp.zeros(u_ref.shape, F32)

    @pl.when(s % PRE_BLOCKS == 0)
    def _norm():
        _norm_tile(h_ref, nw_ref, xn_ref, tm=tm)
        gate_ref[...] = _dot(xn_ref[...], wg_ref[...])

    @pl.when(first)
    def _from_state():
        u_ref[0:8, :] = jnp.zeros((8, A_Q), F32)
        u_ref[8 - a_tail:8, :] = ca0_ref[...]

    @pl.when(jnp.logical_not(first))
    def _from_prev_tile():
        u_ref[0:8, :] = utail_ref[jp]

    def project(r0, rows):
        xs = xn_ref[pl.ds(r0, rows), :]
        for g, w_ref in enumerate((wb_ref, wc_ref, wx_ref, wz_ref)):
            acc_ref[:, g * A_Q:(g + 1) * A_Q] = _dot(xs, w_ref[...])

    def finish(e0, rows):
        rb = raw_ref[prev, pl.ds(8 + e0, rows), :]
        u = rb[:, A_Q:2 * A_Q] * rb[:, 2 * A_Q:3 * A_Q]
        u_ref[pl.ds(8 + e0, rows), :] = u
        conv = _causal_conv(u_ref[pl.ds(e0, 8), :], u, caw_ref, CONV_A_W)
        ya_ref[pl.ds(e0, rows), :] = (rb[:, 0:A_Q] * conv * _silu(rb[:, 3 * A_Q:4 * A_Q])).astype(ya_ref.dtype)

    _project_and_finish(project, finish, raw_ref, acc_ref, tm=tm)
    utail_ref[jp] = u_ref[tm:tm + 8, :]
    ca1_ref[...] = u_ref[8 + tm - a_tail:8 + tm, :]


def _pre_q_kernel(xn_ref, w_ref, cqw_ref, par_ref, cq0_ref, qkvz_ref, cq1_ref,
                  raw_ref, acc_ref, qtail_ref, *, tm, tiles_per_seq):
    s = pl.program_id(0)
    ip, jp = _lagged(s)
    prev = 1 - s % 2
    first = (ip % tiles_per_seq) == 0
    q_tail = CONV_QKV_W - 1

    @pl.when(s == 0)
    def _clear():
        raw_ref[...] = jnp.zeros(raw_ref.shape, F32)

    @pl.when(first)
    def _from_state():
        raw_ref[prev, 0:8, :] = jnp.zeros((8, BN), F32)
        raw_ref[prev, 8 - q_tail:8, :] = cq0_ref[...]

    @pl.when(jnp.logical_not(first))
    def _from_prev_tile():
        raw_ref[prev, 0:8, :] = qtail_ref[jp]

    use_norm = par_ref[0:1, 0:1]
    scale = par_ref[1:2, 0:1]

    def project(r0, rows):
        acc_ref[...] = _dot(xn_ref[pl.ds(r0, rows), :], w_ref[...])

    def finish(e0, rows):
        for hd in range(N_HEADS):
            cols = slice(hd * HEAD_DIM, (hd + 1) * HEAD_DIM)
            y = _silu(_causal_conv(raw_ref[prev, pl.ds(e0, 8), cols], raw_ref[prev, pl.ds(8 + e0, rows), cols],
                                   cqw_ref.at[:, cols], CONV_QKV_W))
            r = lax.rsqrt(jnp.sum(y * y, axis=-1, keepdims=True) + EPS)
            qkvz_ref[pl.ds(e0, rows), cols] = (y * ((1.0 + use_norm * (r - 1.0)) * scale)).astype(qkvz_ref.dtype)

    _project_and_finish(project, finish, raw_ref, acc_ref, tm=tm)
    qtail_ref[jp] = raw_ref[prev, tm:tm + 8, :]

    @pl.when(jp < 3)
    def _state():
        cq1_ref[...] = raw_ref[prev, 8 + tm - q_tail:8 + tm, :]


def _pre(h, norm_w, w_all, w_gate, conv_a_w, conv_q_w, q_par, ca0, cq0, *, seq_len, tm):
    n = h.shape[0]
    assert n % seq_len == 0 and seq_len % tm == 0 and tm % CHUNK == 0
    tps = seq_len // tm
    n_tiles = n // tm
    n_steps = n_tiles * PRE_BLOCKS + 1
    params = pltpu.CompilerParams(dimension_semantics=("arbitrary",), vmem_limit_bytes=VMEM_LIMIT)

    def tile(s):
        return jnp.minimum(s // PRE_BLOCKS, n_tiles - 1)

    def blk(s):
        return s % PRE_BLOCKS

    def a_weight(group):
        return pl.BlockSpec((D_MODEL, A_Q), lambda s: (0, group * PRE_BLOCKS + blk(s)))

    sub = min(SUB_ROWS, tm)
    ya, xn, gates, ca_tiles = pl.pallas_call(
        functools.partial(_pre_a_kernel, tm=tm, tiles_per_seq=tps),
        grid=(n_steps,),
        in_specs=[
            pl.BlockSpec((tm, D_MODEL), lambda s: (tile(s), 0)),
            pl.BlockSpec((1, D_MODEL), lambda s: (0, 0)),
            a_weight(0), a_weight(1), a_weight(2), a_weight(3),
            pl.BlockSpec((D_MODEL, GATE_LANES), lambda s: (0, 0)),
            pl.BlockSpec((CONV_A_W, A_Q), lambda s: (0, _lagged(s)[1])),
            pl.BlockSpec((None, CONV_A_W - 1, A_Q), lambda s: (_lagged(s)[0] // tps, 0, _lagged(s)[1])),
        ],
        out_specs=[
            pl.BlockSpec((tm, A_Q), lambda s: _lagged(s)),
            pl.BlockSpec((tm, D_MODEL), lambda s: (tile(s), 0)),
            pl.BlockSpec((tm, GATE_LANES), lambda s: (tile(s), 0)),
            pl.BlockSpec((None, CONV_A_W - 1, A_Q), lambda s: (_lagged(s)[0], 0, _lagged(s)[1])),
        ],
        out_shape=[
            jax.ShapeDtypeStruct((n, D_A), BF16),
            jax.ShapeDtypeStruct((n, D_MODEL), BF16),
            jax.ShapeDtypeStruct((n, GATE_LANES), F32),
            jax.ShapeDtypeStruct((n_tiles, CONV_A_W - 1, D_A), F32),
        ],
        scratch_shapes=[
            pltpu.VMEM((2, 8 + tm, BN), F32),
            pltpu.VMEM((sub, BN), F32),
            pltpu.VMEM((8 + tm, A_Q), F32),
            pltpu.VMEM((PRE_BLOCKS, 8, A_Q), F32),
        ],
        compiler_params=params,
        name="pre_a",
    )(h, norm_w, w_all, w_all, w_all, w_all, w_gate, conv_a_w, ca0)

    def qkv_blk(s):
        return jnp.minimum(_lagged(s)[1], 2)

    qkvz, cq_tiles = pl.pallas_call(
        functools.partial(_pre_q_kernel, tm=tm, tiles_per_seq=tps),
        grid=(n_steps,),
        in_specs=[
            pl.BlockSpec((tm, D_MODEL), lambda s: (tile(s), 0)),
            pl.BlockSpec((D_MODEL, BN), lambda s: (0, PRE_BLOCKS + blk(s))),
            pl.BlockSpec((CONV_QKV_W, BN), lambda s: (0, _lagged(s)[1])),
            pl.BlockSpec((2, BN), lambda s: (0, _lagged(s)[1])),
            pl.BlockSpec((None, CONV_QKV_W - 1, BN), lambda s: (_lagged(s)[0] // tps, 0, qkv_blk(s))),
        ],
        out_specs=[
            pl.BlockSpec((tm, BN), lambda s: _lagged(s)),
            pl.BlockSpec((None, CONV_QKV_W - 1, BN), lambda s: (_lagged(s)[0], 0, qkv_blk(s))),
        ],
        out_shape=[
            jax.ShapeDtypeStruct((n, D_QKVZ), BF16),
            jax.ShapeDtypeStruct((n_tiles, CONV_QKV_W - 1, D_QKV), F32),
        ],
        scratch_shapes=[
            pltpu.VMEM((2, 8 + tm, BN), F32),
            pltpu.VMEM((sub, BN), F32),
            pltpu.VMEM((PRE_BLOCKS, 8, BN), F32),
        ],
        compiler_params=params,
        name="pre_q",
    )(xn, w_all, conv_q_w, q_par, cq0)
    return ya, qkvz, gates, ca_tiles[tps - 1::tps], cq_tiles[tps - 1::tps]


def _outproj_kernel(ya_ref, yb_ref, h_ref, w_ref, o_ref):
    o_ref[...] = (h_ref[...] + _dot(ya_ref[...], w_ref[0:D_A, :]) + _dot(yb_ref[...], w_ref[D_A:D_MIX, :]))


def _outproj_final_kernel(ya_ref, yb_ref, h_ref, w_ref, fw_ref, o_ref):
    hn = h_ref[...] + _dot(ya_ref[...], w_ref[0:D_A, :]) + _dot(yb_ref[...], w_ref[D_A:D_MIX, :])
    ms = jnp.mean(hn * hn, axis=-1, keepdims=True)
    o_ref[...] = hn * lax.rsqrt(ms + EPS) * fw_ref[...]


def _outproj(ya, yb, h, w_out, final_w, *, tm):
    n = h.shape[0]
    assert n % tm == 0
    row = pl.BlockSpec((tm, D_MODEL), lambda i: (i, 0))
    half = pl.BlockSpec((tm, D_A), lambda i: (i, 0))
    in_specs = [half, half, row, pl.BlockSpec((D_MIX, D_MODEL), lambda i: (0, 0))]
    args = [ya, yb, h, w_out]
    body = _outproj_kernel
    if final_w is not None:
        in_specs.append(pl.BlockSpec((1, D_MODEL), lambda i: (0, 0)))
        args.append(final_w)
        body = _outproj_final_kernel
    return pl.pallas_call(
        body,
        grid=(n // tm,),
        in_specs=in_specs,
        out_specs=row,
        out_shape=jax.ShapeDtypeStruct((n, D_MODEL), F32),
        compiler_params=pltpu.CompilerParams(
            dimension_semantics=("arbitrary",), vmem_limit_bytes=VMEM_LIMIT),
        name="outproj",
    )(*args)


def _block_mask(row, col, size):
    shift = size.bit_length() - 1
    return (row >> shift) == (col >> shift)


def _inv_unit_lower(n_mats, row, col):
    eye = (row == col).astype(F32)
    in8 = _block_mask(row, col, 8)
    nd = [jnp.where(in8, n, 0.0) for n in n_mats]
    nd16 = [x.astype(BF16) for x in nd]
    m2_16 = [_dot(x, x).astype(BF16) for x in nd16]
    p = [eye - x for x in nd]
    p = [x + _dot(x.astype(BF16), m) for x, m in zip(p, m2_16)]
    m4_16 = [_dot(m, m).astype(BF16) for m in m2_16]
    t_mats = [x + _dot(x.astype(BF16), m) for x, m in zip(p, m4_16)]
    size = 8
    while size < CHUNK:
        off = jnp.logical_and(_block_mask(row, col, 2 * size), jnp.logical_not(_block_mask(row, col, size)))
        n_off16 = [jnp.where(off, n, 0.0).astype(BF16) for n in n_mats]
        t16 = [x.astype(BF16) for x in t_mats]
        a16 = [_dot(n, x).astype(BF16) for n, x in zip(n_off16, t16)]
        t_mats = [x - _dot(x16, a) for x, x16, a in zip(t_mats, t16, a16)]
        size *= 2
    return t_mats


def _gdn_kernel(qkvz_ref, gate_ref, s0_ref, gp_ref, onw_ref, y_ref, s1_ref,
                s_scr, u_scr, wq_scr, qk_scr, kdt_scr, gl_scr, *, nb, n_chunks):
    t = pl.program_id(1)

    @pl.when(t == 0)
    def _init():
        s_scr[...] = s0_ref[...]

    row = lax.broadcasted_iota(jnp.int32, (CHUNK, CHUNK), 0)
    col = lax.broadcasted_iota(jnp.int32, (CHUNK, CHUNK), 1)
    causal = row >= col
    strict = row > col
    tril = causal.astype(F32)
    probs = [(b, h) for b in range(nb) for h in range(N_HEADS)]
    n = range(len(probs))

    def head_cols(off, h):
        return slice(off + h * HEAD_DIM, off + (h + 1) * HEAD_DIM)

    def chunk_body(ci, carry):
        r0 = pl.multiple_of(ci * CHUNK, CHUNK)
        rows = pl.ds(r0, CHUNK)

        beta_t, gc, gc_t = [], [], []
        for b in range(nb):
            gl = gate_ref[b, rows, :]
            beta_t.append(_sigmoid(gl))
            g_t = -jnp.exp(gp_ref[0:1, :]) * _softplus(gl + gp_ref[1:2, :])
            gc.append(_dot_hi(tril, g_t))
            gc_t.append(gc[b].T)
            gl_scr[b, pl.ds(ci, 1), :] = gc[b][CHUNK - 1:CHUNK, :]

        q = [qkvz_ref[b, rows, head_cols(0, h)].astype(F32) for b, h in probs]
        k = [qkvz_ref[b, rows, head_cols(D_DELTA, h)].astype(F32) for b, h in probs]
        v = [qkvz_ref[b, rows, head_cols(2 * D_DELTA, h)].astype(F32) for b, h in probs]

        beta_c = [beta_t[b][:, h:h + 1] for b, h in probs]
        gc_c = [gc[b][:, N_HEADS + h:N_HEADS + h + 1] for b, h in probs]
        gc_r = [gc_t[b][N_HEADS + h:N_HEADS + h + 1, :] for b, h in probs]
        g_last = [gc[b][CHUNK - 1:CHUNK, N_HEADS + h:N_HEADS + h + 1] for b, h in probs]
        decay = [jnp.where(causal, jnp.exp(gc_c[p] - gc_r[p]), 0.0) for p in n]
        eg_c = [jnp.exp(gc_c[p]) for p in n]

        kb = [k[p] * beta_c[p] for p in n]
        k_t = [k[p].T for p in n]
        kq = [_dot(jnp.concatenate([kb[p], q[p]], axis=0).astype(BF16), k_t[p].astype(BF16))
              for p in n]
        for p, (b, h) in enumerate(probs):
            qk_scr[b, ci, h] = (kq[p][CHUNK:2 * CHUNK] * decay[p]).astype(BF16)
            kdt_scr[b, ci, h] = (k_t[p] * jnp.exp(g_last[p] - gc_r[p])).astype(BF16)
        t_mat = _inv_unit_lower([jnp.where(strict, kq[p][0:CHUNK] * decay[p], 0.0) for p in n], row, col)

        uw = [_dot(t_mat[p].astype(BF16),
                   jnp.concatenate([v[p] * beta_c[p], kb[p] * eg_c[p]], axis=1).astype(BF16))
              for p in n]
        for p, (b, h) in enumerate(probs):
            u_scr[b, ci, :, head_cols(0, h)] = uw[p][:, 0:HEAD_DIM]
            wq_scr[b, ci, h, 0:CHUNK, :] = uw[p][:, HEAD_DIM:2 * HEAD_DIM].astype(BF16)
            wq_scr[b, ci, h, CHUNK:2 * CHUNK, :] = (q[p] * eg_c[p]).astype(BF16)
        return carry

    def state_body(ci, carry):
        r0 = pl.multiple_of(ci * CHUNK, CHUNK)
        rows = pl.ds(r0, CHUNK)
        eg_last = [jnp.exp(gl_scr[b, pl.ds(ci, 1), :]) for b in range(nb)]
        s = [s_scr[b, h] for b, h in probs]
        ws = [_dot(wq_scr[b, ci, h], s[p].astype(BF16)) for p, (b, h) in enumerate(probs)]
        vn16 = [(u_scr[b, ci, :, head_cols(0, h)] - ws[p][0:CHUNK]).astype(BF16)
                for p, (b, h) in enumerate(probs)]
        for p, (b, h) in enumerate(probs):
            s_scr[b, h] = (s[p] * eg_last[b][:, N_HEADS + h:N_HEADS + h + 1]
                           + _dot(kdt_scr[b, ci, h], vn16[p]))
        o = [ws[p][CHUNK:2 * CHUNK] + _dot(qk_scr[b, ci, h], vn16[p]) for p, (b, h) in enumerate(probs)]
        for p, (b, h) in enumerate(probs):
            ms = jnp.mean(o[p] * o[p], axis=-1, keepdims=True)
            on = o[p] * lax.rsqrt(ms + EPS) * onw_ref[...]
            gate = qkvz_ref[b, rows, head_cols(D_QKV, h)].astype(F32)
            y_ref[b, rows, head_cols(0, h)] = (on * gate).astype(y_ref.dtype)
        return carry

    lax.fori_loop(0, n_chunks, chunk_body, 0)
    lax.fori_loop(0, n_chunks, state_body, 0)

    @pl.when(t == pl.num_programs(1) - 1)
    def _fin():
        s1_ref[...] = s_scr[...]


def _gdn(qkvz, gates, s0, gate_par, o_norm_w, *, nb, tile):
    b, l, _ = qkvz.shape
    assert b % nb == 0 and l % tile == 0 and tile % CHUNK == 0
    n_chunks = tile // CHUNK

    def seq(width):
        return pl.BlockSpec((nb, tile, width), lambda i, t: (i, t, 0))

    st_s = (N_HEADS, HEAD_DIM, HEAD_DIM)
    state = pl.BlockSpec((nb,) + st_s, lambda i, t: (i, 0, 0, 0))
    return pl.pallas_call(
        functools.partial(_gdn_kernel, nb=nb, n_chunks=n_chunks),
        grid=(b // nb, l // tile),
        in_specs=[
            seq(D_QKVZ), seq(GATE_LANES), state,
            pl.BlockSpec((2, GATE_LANES), lambda i, t: (0, 0)),
            pl.BlockSpec((1, HEAD_DIM), lambda i, t: (0, 0)),
        ],
        out_specs=[seq(D_DELTA), state],
        out_shape=[
            jax.ShapeDtypeStruct((b, l, D_DELTA), BF16),
            jax.ShapeDtypeStruct((b,) + st_s, F32),
        ],
        scratch_shapes=[
            pltpu.VMEM((nb,) + st_s, F32),
            pltpu.VMEM((nb, n_chunks, CHUNK, D_DELTA), F32),
            pltpu.VMEM((nb, n_chunks, N_HEADS, 2 * CHUNK, HEAD_DIM), BF16),
            pltpu.VMEM((nb, n_chunks, N_HEADS, CHUNK, CHUNK), BF16),
            pltpu.VMEM((nb, n_chunks, N_HEADS, HEAD_DIM, CHUNK), BF16),
            pltpu.VMEM((nb, max(n_chunks, 8), GATE_LANES), F32),
        ],
        compiler_params=pltpu.CompilerParams(
            dimension_semantics=("arbitrary", "arbitrary"), vmem_limit_bytes=VMEM_LIMIT),
        name="gdn",
    )(qkvz, gates, s0, gate_par, o_norm_w)


def _pick_tile(n, target):
    t = min(n, target)
    while n % t:
        t //= 2
    return t


def kernel(x_prompt, x_sample, state_conv_a, state_conv_qkv, state_delta, meta_tokens, norm_w, w_in,
           conv_a_w, conv_qkv_w, a_log, dt_bias, o_norm_w, w_out, final_norm_w):
    b, seq, d = x_prompt.shape
    bs, dec_seq, _ = x_sample.shape
    depth = norm_w.shape[0]
    assert d == D_MODEL and dec_seq == CHUNK and seq % CHUNK == 0

    meta_blk = jnp.concatenate([jnp.zeros((CHUNK - N_META, d), F32), meta_tokens.astype(F32)], axis=0)
    h_small = jnp.concatenate([jnp.broadcast_to(meta_blk[None], (b, CHUNK, d)), x_sample], axis=0)
    h_small = h_small.reshape((b + bs) * CHUNK, d)
    h_main = x_prompt.reshape(b * seq, d)
    n_small, n_main = h_small.shape[0], h_main.shape[0]

    zb_taps = jnp.zeros((CONV_QKV_W, D_DELTA), F32).at[CONV_QKV_W - 1].set(1.0)
    q_par = jnp.stack([
        jnp.concatenate([jnp.ones((2 * D_DELTA,), F32), jnp.zeros((2 * D_DELTA,), F32)]),
        jnp.concatenate([jnp.full((D_DELTA,), HEAD_DIM ** -0.5, F32), jnp.ones((3 * D_DELTA,), F32)]),
    ])

    p_a, p_qkv, p_s, s_a, s_qkv, s_s = [], [], [], [], [], []
    for layer in range(depth):
        w_all = w_in[layer, :, :D_MAIN].astype(BF16)
        w_gate = jnp.pad(w_in[layer, :, D_MAIN:], ((0, 0), (0, GATE_LANES - 2 * N_HEADS))).astype(BF16)
        w_o = w_out[layer].astype(BF16)
        nw = norm_w[layer].reshape(1, d)
        gate_par = jnp.zeros((2, GATE_LANES), F32)
        gate_par = gate_par.at[0, N_HEADS:2 * N_HEADS].set(a_log[layer])
        gate_par = gate_par.at[1, N_HEADS:2 * N_HEADS].set(dt_bias[layer])
        onw = o_norm_w[layer].reshape(1, HEAD_DIM)
        fw = final_norm_w.reshape(1, d) if layer == depth - 1 else None
        pre = functools.partial(_pre, norm_w=nw, w_all=w_all, w_gate=w_gate, conv_a_w=conv_a_w[layer],
                                conv_q_w=jnp.concatenate([conv_qkv_w[layer], zb_taps], axis=1), q_par=q_par)
        gdn = functools.partial(_gdn, gate_par=gate_par, o_norm_w=onw)

        ca0 = jnp.concatenate([jnp.zeros((b,) + state_conv_a.shape[2:], F32), state_conv_a[layer]], axis=0)
        cq0 = jnp.concatenate([jnp.zeros((b,) + state_conv_qkv.shape[2:], F32), state_conv_qkv[layer]], axis=0)
        s0 = jnp.concatenate([jnp.zeros((b,) + state_delta.shape[2:], F32), state_delta[layer]], axis=0)
        ya, qkvz, gates, ca1, cq1 = pre(h_small, ca0=ca0, cq0=cq0, seq_len=CHUNK, tm=CHUNK)
        yb, s1 = gdn(qkvz.reshape(b + bs, CHUNK, D_QKVZ), gates.reshape(b + bs, CHUNK, GATE_LANES), s0,
                     nb=_pick_tile(b + bs, GDN_BATCH), tile=CHUNK)
        h_small = _outproj(ya, yb.reshape(n_small, D_DELTA), h_small, w_o, fw, tm=_pick_tile(n_small, 512))
        s_a.append(ca1[b:]); s_qkv.append(cq1[b:]); s_s.append(s1[b:])

        ya, qkvz, gates, ca2, cq2 = pre(h_main, ca0=ca1[:b], cq0=cq1[:b], seq_len=seq,
                                        tm=_pick_tile(seq, 1024))
        yb, s2 = gdn(qkvz.reshape(b, seq, D_QKVZ), gates.reshape(b, seq, GATE_LANES), s1[:b],
                     nb=_pick_tile(b, GDN_BATCH), tile=_pick_tile(seq, GDN_TILE))
        h_main = _outproj(ya, yb.reshape(n_main, D_DELTA), h_main, w_o, fw, tm=_pick_tile(n_main, 512))
        p_a.append(ca2); p_qkv.append(cq2); p_s.append(s2)

    y_prompt = h_main.reshape(b, seq, d)
    y_sample = h_small.reshape(b + bs, CHUNK, d)[b:]
    return (y_prompt, y_sample, jnp.stack(p_a), jnp.stack(p_qkv), jnp.stack(p_s),
            jnp.stack(s_a), jnp.stack(s_qkv), jnp.stack(s_s))
```
